```python
import math
import jax, jax.numpy as jnp
from jax import lax
import numpy as np

D_MODEL = 2048
BATCH = 4
SEQ = 2048
DEPTH = 1
DEC_BATCH = 128
DEC_SEQ = 8
PAST_LEN = 16384
PAGE_SIZE = 128

N_META = 16
MLA_HEADS = 8
Q_LORA = 512
KV_LORA = 256
NOPE_DIM = 128
ROPE_DIM = 64
QK_DIM = NOPE_DIM + ROPE_DIM
MLA_V_DIM = 128
ROPE_THETA = 10000.0
DIFF_HEADS = 8
DIFF_KV_HEADS = 2
DIFF_HALF_DIM = 64
DIFF_V_DIM = 2 * DIFF_HALF_DIM
HEADS_PER_KV = DIFF_HEADS // DIFF_KV_HEADS
MIX_WIDTH = MLA_HEADS * MLA_V_DIM + DIFF_HEADS * DIFF_V_DIM
N_BUCKETS = 32
MAX_DISTANCE = 128
N_EXPERTS = 32
TOP_K = 4
D_FF = D_MODEL
SWIGLU_LIMIT = 7.0
SWIGLU_ALPHA = 1.702
EPS = 1e-6
Q_BLOCK = 128
MLA_SCALE = QK_DIM ** -0.5
DIFF_SCALE = DIFF_HALF_DIM ** -0.5
N_QA = Q_LORA
N_KV = KV_LORA
N_KR = ROPE_DIM
N_DQ = DIFF_HEADS * 2 * DIFF_HALF_DIM
N_DK = DIFF_KV_HEADS * 2 * DIFF_HALF_DIM
N_DV = DIFF_KV_HEADS * DIFF_V_DIM
N_IN = N_QA + N_KV + N_KR + N_DQ + N_DK + N_DV

kernel_name = "hymba_mla_diffattn_t5bias_moe_step"

F32 = jnp.float32


def rmsnorm(x, g):
    xf = x.astype(F32)
    y = xf * lax.rsqrt(jnp.mean(xf * xf, axis=-1, keepdims=True) + EPS) * g.astype(F32)
    return y.astype(x.dtype)


def rope(x, pos):
    half = x.shape[-1] // 2
    inv = jnp.power(ROPE_THETA, -jnp.arange(half, dtype=F32) / half)
    ang = pos.astype(F32)[:, None] * inv[None, :]
    cos = jnp.cos(ang)[:, None, :]
    sin = jnp.sin(ang)[:, None, :]
    xf = x.astype(F32)
    x1, x2 = xf[..., :half], xf[..., half:]
    return jnp.concatenate([x1 * cos - x2 * sin, x1 * sin + x2 * cos], axis=-1).astype(x.dtype)


def t5_bucket(rel):
    n = jnp.maximum(rel, 0)
    max_exact = N_BUCKETS // 2
    nf = jnp.maximum(n, 1).astype(F32)
    large = max_exact + (jnp.log(nf / max_exact) / math.log(MAX_DISTANCE / max_exact)
                         * (N_BUCKETS - max_exact)).astype(jnp.int32)
    large = jnp.minimum(large, N_BUCKETS - 1)
    return jnp.where(n < max_exact, n, large)


def project(h, pos, g_attn_l, w_in_l, g_qa_l, w_qb_l, g_kv_l, g_q_l, g_dq_l, g_dk_l):
    p = rmsnorm(h, g_attn_l) @ w_in_l
    o = np.cumsum([0, N_QA, N_KV, N_KR, N_DQ, N_DK, N_DV])
    q_a, kv, kr, dq, dk, dv = [p[..., o[i]:o[i + 1]] for i in range(6)]
    q = jnp.einsum('...r,rhd->...hd', rmsnorm(q_a, g_qa_l), w_qb_l)
    q = rmsnorm(q, g_q_l)
    q_mla = jnp.concatenate([q[..., :NOPE_DIM], rope(q[..., NOPE_DIM:], pos)], axis=-1)
    c_kv = rmsnorm(kv, g_kv_l)
    lead = p.shape[:-1]
    qd = rmsnorm(dq.reshape(lead + (DIFF_HEADS, 2, DIFF_HALF_DIM)), g_dq_l)
    kd = rmsnorm(dk.reshape(lead + (DIFF_KV_HEADS, 2, DIFF_HALF_DIM)), g_dk_l)
    vd = dv.reshape(lead + (DIFF_KV_HEADS, DIFF_V_DIM))
    return q_mla, c_kv, kr, qd, kd, vd


def mla_keys(c, k_rope, pos, w_uk, g_k):
    k_nope = jnp.einsum('...tr,rhd->...thd', c, w_uk)
    k_pe = jnp.broadcast_to(k_rope[..., None, :], k_nope.shape[:-1] + (ROPE_DIM,))
    k = rmsnorm(jnp.concatenate([k_nope, k_pe], axis=-1), g_k)
    return jnp.concatenate([k[..., :NOPE_DIM], rope(k[..., NOPE_DIM:], pos)], axis=-1)


def mla_core(q, q_pos, k, c, k_pos, w_uv):
    s = jnp.einsum('shd,thd->hst', q.astype(F32), k.astype(F32)) * MLA_SCALE
    s = jnp.where(k_pos[None, None, :] <= q_pos[None, :, None], s, -jnp.inf)
    p = jax.nn.softmax(s, axis=-1)
    o_lat = jnp.einsum('hst,tr->shr', p, c.astype(F32))
    return jnp.einsum('shr,rhd->shd', o_lat, w_uv.astype(F32))


def diff_core(q, q_pos, k, v, k_pos, rel_bias, lam):
    S = q.shape[0]
    qg = q.reshape(S, DIFF_KV_HEADS, HEADS_PER_KV, 2, DIFF_HALF_DIM).astype(F32)
    s = jnp.einsum('sgrmd,tgmd->grmst', qg, k.astype(F32)) * DIFF_SCALE
    bias = rel_bias.astype(F32)[t5_bucket(q_pos[:, None] - k_pos[None, :])]
    bias = bias.transpose(2, 0, 1).reshape(DIFF_KV_HEADS, HEADS_PER_KV, S, -1)
    s = s + bias[:, :, None]
    s = jnp.where(k_pos[None, :] <= q_pos[:, None], s, -jnp.inf)
    p = jax.nn.softmax(s, axis=-1)
    pd = p[:, :, 0] - lam * p[:, :, 1]
    o = jnp.einsum('grst,tgd->sgrd', pd, v.astype(F32))
    return o.reshape(S, DIFF_HEADS, DIFF_V_DIM)


def moe(h, l, g_ffn, w_router, b_router, w_gate_up, b_gate_up, w_down, b_down):
    hn = rmsnorm(h, g_ffn[l])
    logits = hn.astype(F32) @ w_router[l].astype(F32) + b_router[l].astype(F32)
    top_v, top_i = lax.top_k(logits, TOP_K)
    wts = jax.nn.softmax(top_v, axis=-1)
    gates = jnp.sum(jax.nn.one_hot(top_i, N_EXPERTS, dtype=F32) * wts[..., None], axis=-2)
    out = jnp.zeros(h.shape, F32)
    for e in range(N_EXPERTS):
        gu = hn @ w_gate_up[l, e] + b_gate_up[l, e]
        gate = jnp.minimum(gu[..., :D_FF], SWIGLU_LIMIT)
        up = jnp.clip(gu[..., D_FF:], -SWIGLU_LIMIT, SWIGLU_LIMIT)
        act = (up + 1) * gate * jax.nn.sigmoid(SWIGLU_ALPHA * gate)
        out = out + gates[:, e:e + 1] * (act @ w_down[l, e] + b_down[l, e]).astype(F32)
    return out.astype(h.dtype)


def setup_inputs(seed: int = 0) -> dict:
    key = jax.random.key(seed)
    ks = jax.random.split(key, 40)
    n_pages = PAST_LEN // PAGE_SIZE
    n_used = DEC_BATCH * n_pages
    n_pool = (n_used * 5) // 4
    nrm = lambda k, shape, s=1.0: jax.random.normal(k, shape, F32) * s
    gain = lambda k, shape: 1.0 + 0.1 * jax.random.normal(k, shape, F32)
    page_table = jax.random.permutation(ks[6], n_pool)[:n_used].reshape(DEC_BATCH, n_pages).astype(jnp.int32)
    return {
        "x_prompt": nrm(ks[0], (BATCH, SEQ, D_MODEL)),
        "x_sample": nrm(ks[1], (DEC_BATCH, DEC_SEQ, D_MODEL)),
        "cache_mla_latent": nrm(ks[2], (DEPTH, n_pool, PAGE_SIZE, KV_LORA)),
        "cache_mla_krope": nrm(ks[3], (DEPTH, n_pool, PAGE_SIZE, ROPE_DIM)),
        "cache_diff_k": nrm(ks[4], (DEPTH, n_pool, PAGE_SIZE, DIFF_KV_HEADS, 2 * DIFF_HALF_DIM)),
        "cache_diff_v": nrm(ks[5], (DEPTH, n_pool, PAGE_SIZE, DIFF_KV_HEADS, DIFF_V_DIM)),
        "page_table": page_table,
        "meta_tokens": nrm(ks[7], (N_META, D_MODEL)),
        "rel_bias": nrm(ks[8], (N_BUCKETS, DIFF_HEADS), 0.5),
        "g_attn": gain(ks[9], (DEPTH, D_MODEL)),
        "w_in": nrm(ks[10], (DEPTH, D_MODEL, N_IN), D_MODEL ** -0.5),
        "g_qa": gain(ks[11], (DEPTH, Q_LORA)),
        "w_qb": nrm(ks[12], (DEPTH, Q_LORA, MLA_HEADS, QK_DIM), Q_LORA ** -0.5),
        "g_kv": gain(ks[13], (DEPTH, KV_LORA)),
        "w_kvb": nrm(ks[14], (DEPTH, KV_LORA, MLA_HEADS, NOPE_DIM + MLA_V_DIM), KV_LORA ** -0.5),
        "g_mla_q": gain(ks[15], (DEPTH, QK_DIM)),
        "g_mla_k": gain(ks[16], (DEPTH, QK_DIM)),
        "g_diff_q": gain(ks[17], (DEPTH, DIFF_HALF_DIM)),
        "g_diff_k": gain(ks[18], (DEPTH, DIFF_HALF_DIM)),
        "lambda_q1": nrm(ks[19], (DEPTH, DIFF_HALF_DIM), 0.1),
        "lambda_k1": nrm(ks[20], (DEPTH, DIFF_HALF_DIM), 0.1),
        "lambda_q2": nrm(ks[21], (DEPTH, DIFF_HALF_DIM), 0.1),
        "lambda_k2": nrm(ks[22], (DEPTH, DIFF_HALF_DIM), 0.1),
        "g_subln": gain(ks[23], (DEPTH, DIFF_V_DIM)),
        "w_o": nrm(ks[24], (DEPTH, MIX_WIDTH, D_MODEL), MIX_WIDTH ** -0.5),
        "g_ffn": gain(ks[25], (DEPTH, D_MODEL)),
        "w_router": nrm(ks[26], (DEPTH, D_MODEL, N_EXPERTS), D_MODEL ** -0.5),
        "b_router": nrm(ks[27], (DEPTH, N_EXPERTS), 0.01),
        "w_gate_up": nrm(ks[28], (DEPTH, N_EXPERTS, D_MODEL, 2 * D_FF), D_MODEL ** -0.5),
        "b_gate_up": nrm(ks[29], (DEPTH, N_EXPERTS, 2 * D_FF), 0.01),
        "w_down": nrm(ks[30], (DEPTH, N_EXPERTS, D_FF, D_MODEL), D_FF ** -0.5),
        "b_down": nrm(ks[31], (DEPTH, N_EXPERTS, D_MODEL), 0.01),
    }


def reference(x_prompt, x_sample, cache_mla_latent, cache_mla_krope, cache_diff_k, cache_diff_v, page_table,
              meta_tokens, rel_bias, g_attn, w_in, g_qa, w_qb, g_kv, w_kvb, g_mla_q, g_mla_k, g_diff_q, g_diff_k,
              lambda_q1, lambda_k1, lambda_q2, lambda_k2, g_subln, w_o, g_ffn, w_router, b_router,
              w_gate_up, b_gate_up, w_down, b_down):
    B = x_prompt.shape[0]
    DB, S = x_sample.shape[0], x_sample.shape[1]
    meta = jnp.broadcast_to(meta_tokens.astype(x_prompt.dtype)[None], (B, N_META, D_MODEL))
    h_p = jnp.concatenate([meta, x_prompt], axis=1)
    h_s = x_sample
    L = h_p.shape[1]
    n_qb = -(-L // Q_BLOCK)
    Lp = n_qb * Q_BLOCK
    pos_p = jnp.arange(L, dtype=jnp.int32)
    qpos_blocks = jnp.arange(Lp, dtype=jnp.int32).reshape(n_qb, Q_BLOCK)
    pos_sq = PAST_LEN + jnp.arange(S, dtype=jnp.int32)
    pos_sk = jnp.arange(PAST_LEN + S, dtype=jnp.int32)

    def to_blocks(x):
        x = jnp.pad(x, [(0, 0), (0, Lp - L)] + [(0, 0)] * (x.ndim - 2))
        return x.reshape((B, n_qb, Q_BLOCK) + x.shape[2:]).swapaxes(0, 1)

    def from_blocks(x):
        x = x.swapaxes(0, 1)
        return x.reshape((B, Lp) + x.shape[3:])[:, :L]

    lat_p, kr_p, dk_p, dv_p, lat_s, kr_s, dk_s, dv_s = [], [], [], [], [], [], [], []
    for l in range(DEPTH):
        lam_init = 0.8 - 0.6 * math.exp(-0.3 * l)
        lam = (jnp.exp(jnp.sum(lambda_q1[l].astype(F32) * lambda_k1[l].astype(F32)))
               - jnp.exp(jnp.sum(lambda_q2[l].astype(F32) * lambda_k2[l].astype(F32))) + lam_init)
        w_uk = w_kvb[l][..., :NOPE_DIM]
        w_uv = w_kvb[l][..., NOPE_DIM:]
        prm = (g_attn[l], w_in[l], g_qa[l], w_qb[l], g_kv[l], g_mla_q[l], g_diff_q[l], g_diff_k[l])
        g_k = g_mla_k[l]

        qm_p, c_p, kr_pr, qd_p, kd_p, vd_p = project(h_p, pos_p, *prm)
        km_p = mla_keys(c_p, kr_pr, pos_p, w_uk, g_k)

        def prompt_block(args):
            qm_b, qd_b, qpos_b = args
            om = jax.vmap(mla_core, in_axes=(0, None, 0, 0, None, None))(qm_b, qpos_b, km_p, c_p, pos_p, w_uv)
            od = jax.vmap(diff_core, in_axes=(0, None, 0, 0, None, None, None))(
                qd_b, qpos_b, kd_p, vd_p, pos_p, rel_bias, lam)
            return om, od

        om_p, od_p = lax.map(prompt_block, (to_blocks(qm_p), to_blocks(qd_p), qpos_blocks))
        om_p, od_p = from_blocks(om_p), from_blocks(od_p)

        qm_s, c_s, kr_sn, qd_s, kd_s, vd_s = project(h_s, pos_sq, *prm)
        kd_s_flat = kd_s.reshape(DB, S, DIFF_KV_HEADS, 2 * DIFF_HALF_DIM)

        def sample_seq(args):
            pt, qm, c_new, kr_new, qd, kd_new, vd_new = args
            gather = lambda pool: pool[l, pt].reshape((-1,) + pool.shape[3:])
            c = jnp.concatenate([gather(cache_mla_latent), c_new], axis=0)
            kr = jnp.concatenate([gather(cache_mla_krope), kr_new], axis=0)
            k = mla_keys(c, kr, pos_sk, w_uk, g_k)
            om = mla_core(qm, pos_sq, k, c, pos_sk, w_uv)
            kd = jnp.concatenate([gather(cache_diff_k), kd_new], axis=0)
            kd = kd.reshape(kd.shape[0], DIFF_KV_HEADS, 2, DIFF_HALF_DIM)
            vd = jnp.concatenate([gather(cache_diff_v), vd_new], axis=0)
            od = diff_core(qd, pos_sq, kd, vd, pos_sk, rel_bias, lam)
            return om, od

        om_s, od_s = lax.map(sample_seq, (page_table, qm_s, c_s, kr_sn, qd_s, kd_s_flat, vd_s))

        def merge(om, od, h):
            od = rmsnorm(od, g_subln[l]) * (1.0 - lam_init)
            cat = jnp.concatenate([om.reshape(om.shape[:-2] + (MLA_HEADS * MLA_V_DIM,)),
                                   od.reshape(od.shape[:-2] + (DIFF_HEADS * DIFF_V_DIM,))], axis=-1)
            return h + cat.astype(h.dtype) @ w_o[l]

        h_p = merge(om_p, od_p, h_p)
        h_s = merge(om_s, od_s, h_s)

        flat = jnp.concatenate([h_p.reshape(-1, D_MODEL), h_s.reshape(-1, D_MODEL)], axis=0)
        ff = moe(flat, l, g_ffn, w_router, b_router, w_gate_up, b_gate_up, w_down, b_down)
        h_p = h_p + ff[:B * L].reshape(B, L, D_MODEL)
        h_s = h_s + ff[B * L:].reshape(DB, S, D_MODEL)

        lat_p.append(c_p); kr_p.append(kr_pr)
        dk_p.append(kd_p.reshape(B, L, DIFF_KV_HEADS, 2 * DIFF_HALF_DIM)); dv_p.append(vd_p)
        lat_s.append(c_s); kr_s.append(kr_sn); dk_s.append(kd_s_flat); dv_s.append(vd_s)

    y_prompt = h_p[:, N_META:]
    y_sample = h_s
    return (y_prompt, y_sample,
            jnp.stack(lat_p), jnp.stack(kr_p), jnp.stack(dk_p), jnp.stack(dv_p),
            jnp.stack(lat_s), jnp.stack(kr_s), jnp.stack(dk_s), jnp.stack(dv_s))
```

```python
import functools
import math

import numpy as np
import jax
import jax.numpy as jnp
from jax import lax
from jax.experimental import pallas as pl
from jax.experimental.pallas import tpu as pltpu

F32 = jnp.float32
BF16 = jnp.bfloat16
NEG_INF = float("-inf")

N_META = 16
ROPE_THETA = 10000.0
N_BUCKETS = 32
MAX_DISTANCE = 128
TOP_K = 4
SWIGLU_LIMIT = 7.0
SWIGLU_ALPHA = 1.702
EPS = 1e-6
PAGE = 128
NOPE = 128
ROPE = 64
QK = NOPE + ROPE
HALF = 64

LANES = 128
TM = 256
VMEM_LIMIT = 56 * 1024 * 1024

CAP = 2048
RS = 256
FC = 256
TC = 128

NT_DIMS = (((1,), (1,)), ((), ()))


def _cparams(sem):
    return pltpu.CompilerParams(dimension_semantics=sem, vmem_limit_bytes=VMEM_LIMIT)


def _dot(a, b):
    return jnp.dot(a, b, preferred_element_type=F32)


def _dot_nt(a, b):
    return lax.dot_general(a, b, NT_DIMS, preferred_element_type=F32)


def _split_bf16(x):
    hi = x.astype(BF16)
    lo = (x - hi.astype(F32)).astype(BF16)
    return hi, lo


def _seg_rsqrt(sq, seg_ref, exp_ref, width, extra=None):
    ssq = _dot(sq.astype(BF16), seg_ref[...])
    if extra is not None:
        ssq = ssq + extra
    rs = lax.rsqrt(ssq * (1.0 / width) + EPS)
    hi, lo = _split_bf16(rs)
    return _dot(hi, exp_ref[...]) + _dot(lo, exp_ref[...])


def _rms_rows(x, width):
    return lax.rsqrt(jnp.sum(x * x, axis=-1, keepdims=True) * (1.0 / width) + EPS)


def _bucket_np(n):
    n = np.maximum(n, 0)
    max_exact = N_BUCKETS // 2
    nf = np.maximum(n, 1).astype(np.float32)
    large = max_exact + (np.log(nf / max_exact) / math.log(MAX_DISTANCE / max_exact)
                         * (N_BUCKETS - max_exact)).astype(np.int32)
    large = np.minimum(large, N_BUCKETS - 1)
    return np.where(n < max_exact, n, large).astype(np.int32)


def _prep_kernel(bk_ref, rt_ref, lq1_ref, lk1_ref, lq2_ref, lk2_ref, tile_ref, lam_ref, *, lam_init):
    bk = bk_ref[0]
    rt = rt_ref[0]
    far = rt[:, N_BUCKETS - 1:N_BUCKETS]
    acc = jnp.zeros((PAGE, LANES), F32)
    for b in range(N_BUCKETS):
        acc = jnp.where(bk == b, rt[:, b:b + 1] - far, acc)
    tile_ref[0] = jnp.where(bk < 0, NEG_INF, acc)
    s1 = jnp.sum(lq1_ref[...] * lk1_ref[...], axis=-1, keepdims=True)
    s2 = jnp.sum(lq2_ref[...] * lk2_ref[...], axis=-1, keepdims=True)
    lam = jnp.exp(s1) - jnp.exp(s2) + lam_init
    lam_ref[...] = jnp.broadcast_to(lam, lam_ref.shape)


def _bias_prep(bk, rt, lq1, lk1, lq2, lk2, lam_init):
    n = bk.shape[0]
    vec = pl.BlockSpec((1, HALF), lambda i: (0, 0))
    return pl.pallas_call(
        functools.partial(_prep_kernel, lam_init=lam_init),
        grid=(n,),
        in_specs=[pl.BlockSpec((1, PAGE, LANES), lambda i: (i, 0, 0)),
                  pl.BlockSpec((1, PAGE, N_BUCKETS), lambda i: (i, 0, 0)),
                  vec, vec, vec, vec],
        out_specs=[pl.BlockSpec((1, PAGE, LANES), lambda i: (i, 0, 0)),
                   pl.BlockSpec((8, LANES), lambda i: (0, 0))],
        out_shape=[jax.ShapeDtypeStruct((n, PAGE, LANES), F32),
                   jax.ShapeDtypeStruct((8, LANES), F32)],
        compiler_params=_cparams(("arbitrary",)),
        name="bias_prep",
    )(bk, rt, lq1, lk1, lq2, lk2)


def _proj_kernel(x_ref, ct_ref, st_ref, gattn_ref, win_ref, gqa_ref, wq_ref, gq_ref, gkv_ref, wuk_ref,
                 gkn_ref, gkr_ref, gdq_ref, gdk_ref,
                 segq_ref, expq_ref, segk_ref, expk_ref, segdq_ref, expdq_ref, segdk_ref, expdk_ref,
                 c_ref, kr_ref, kd_ref, vd_ref, qm_ref, km_ref, cb_ref, qd_ref, kdb_ref, vdb_ref,
                 *, d_model, q_lora, kv_lora, n_heads, n_dheads, n_kvheads):
    x = x_ref[...]
    ct = ct_ref[...]
    st = st_ref[...]
    xn = x * _rms_rows(x, d_model) * gattn_ref[...]
    p = _dot(xn.astype(BF16), win_ref[...])
    o_kv = q_lora
    o_kr = o_kv + kv_lora
    o_dq = o_kr + LANES
    o_dk = o_dq + n_dheads * LANES
    o_dv = o_dk + n_kvheads * LANES
    qa = p[:, :q_lora]
    kv = p[:, o_kv:o_kr]
    krd = p[:, o_kr:o_dq]
    dq = p[:, o_dq:o_dk]
    dk = p[:, o_dk:o_dv]
    dv = p[:, o_dv:o_dv + n_kvheads * LANES]

    def rope(v):
        return v * ct + pltpu.roll(v, 32, 1) * st

    qan = qa * _rms_rows(qa, q_lora) * gqa_ref[...]
    q = _dot(qan.astype(BF16), wq_ref[...])
    qg = q * _seg_rsqrt(q * q, segq_ref, expq_ref, QK) * gq_ref[...]
    pieces = []
    for h in range(n_heads):
        pieces.append(qg[:, h * 256:h * 256 + LANES])
        pieces.append(rope(qg[:, h * 256 + LANES:(h + 1) * 256]))
    qm_ref[...] = jnp.concatenate(pieces, axis=1).astype(BF16)

    c = kv * _rms_rows(kv, kv_lora) * gkv_ref[...]
    c_ref[...] = c
    cb = c.astype(BF16)
    cb_ref[...] = cb
    kr = krd[:, :ROPE]
    kr_ref[...] = kr

    kn = _dot(cb, wuk_ref[...])
    kr_ssq = jnp.sum(kr * kr, axis=-1, keepdims=True)
    rk = _seg_rsqrt(kn * kn, segk_ref, expk_ref, QK, extra=kr_ssq)
    kng = kn * rk * gkn_ref[...]
    krr = rope(krd * gkr_ref[...])
    pieces = []
    for h in range(n_heads):
        pieces.append(kng[:, h * LANES:(h + 1) * LANES])
        pieces.append(krr * rk[:, h * LANES:(h + 1) * LANES])
    km_ref[...] = jnp.concatenate(pieces, axis=1).astype(BF16)

    qd = dq * _seg_rsqrt(dq * dq, segdq_ref, expdq_ref, HALF) * gdq_ref[...]
    lane = lax.broadcasted_iota(jnp.int32, (x.shape[0], LANES), 1)
    pieces = []
    for h in range(n_dheads):
        blk = qd[:, h * LANES:(h + 1) * LANES]
        pieces.append(jnp.where(lane < HALF, blk, 0.0))
        pieces.append(jnp.where(lane >= HALF, blk, 0.0))
    qd_ref[...] = jnp.concatenate(pieces, axis=1).astype(BF16)
    kd = dk * _seg_rsqrt(dk * dk, segdk_ref, expdk_ref, HALF) * gdk_ref[...]
    kd_ref[...] = kd
    kdb_ref[...] = kd.astype(BF16)
    vd_ref[...] = dv
    vdb_ref[...] = dv.astype(BF16)


def _seg_mats(width, seg, n_seg, skip=None):
    m = np.zeros((width, LANES), np.float32)
    for i in range(width):
        s = i // seg
        if s < n_seg and not (skip is not None and skip(i)):
            m[i, s] = 1.0
    e = np.zeros((LANES, width), np.float32)
    for i in range(width):
        s = i // seg
        if s < n_seg:
            e[s, i] = 1.0
    return jnp.asarray(m, BF16), jnp.asarray(e, BF16)


def _project(h_flat, ct, st, prm, dims):
    nf, d_model = h_flat.shape
    n_heads, n_dheads, n_kvheads = dims["n_heads"], dims["n_dheads"], dims["n_kvheads"]
    q_lora, kv_lora = dims["q_lora"], dims["kv_lora"]
    n_in = prm["w_in"].shape[1]
    row = lambda w: pl.BlockSpec((TM, w), lambda i: (i, 0))
    full = lambda a: pl.BlockSpec(a.shape, lambda i: (0,) * a.ndim)
    consts = [prm["g_attn"], prm["w_in"], prm["g_qa"], prm["w_q"], prm["g_q"], prm["g_kv"], prm["w_uk"],
              prm["g_kn"], prm["g_kr"], prm["g_dq"], prm["g_dk"],
              prm["segq"], prm["expq"], prm["segk"], prm["expk"], prm["segdq"], prm["expdq"],
              prm["segdk"], prm["expdk"]]
    kvw = n_kvheads * LANES
    out_w = [(kv_lora, F32), (ROPE, F32), (kvw, F32), (kvw, F32),
             (n_heads * 256, BF16), (n_heads * 256, BF16), (kv_lora, BF16),
             (n_dheads * 256, BF16), (kvw, BF16), (kvw, BF16)]
    return pl.pallas_call(
        functools.partial(_proj_kernel, d_model=d_model, q_lora=q_lora, kv_lora=kv_lora,
                          n_heads=n_heads, n_dheads=n_dheads, n_kvheads=n_kvheads),
        grid=(nf // TM,),
        in_specs=[row(d_model), row(LANES), row(LANES)] + [full(a) for a in consts],
        out_specs=[row(w) for w, _ in out_w],
        out_shape=[jax.ShapeDtypeStruct((nf, w), dt) for w, dt in out_w],
        compiler_params=_cparams(("arbitrary",)),
        name="project",
    )(h_flat, ct, st, *consts)


def _sample_q_kernel(qm_ref, gkn_ref, wukt_ref, perm_ref, qabs_ref, qf_ref, *, n_heads):
    for h in range(n_heads):
        qn = qm_ref[:, h * 256:h * 256 + LANES].astype(F32) * gkn_ref[...]
        qabs_ref[:, h * 256:(h + 1) * 256] = _dot(qn.astype(BF16), wukt_ref[h]).astype(BF16)
        qr = qm_ref[:, h * 256 + LANES:(h + 1) * 256]
        qf_ref[:, h * LANES:(h + 1) * LANES] = _dot(qr, perm_ref[...]).astype(BF16)


def _sample_q(qm, gkn, wukt, perm, row0, ns, n_heads):
    blk0 = row0 // LANES
    return pl.pallas_call(
        functools.partial(_sample_q_kernel, n_heads=n_heads),
        grid=(ns // LANES,),
        in_specs=[pl.BlockSpec((LANES, n_heads * 256), lambda i: (blk0 + i, 0)),
                  pl.BlockSpec(gkn.shape, lambda i: (0, 0)),
                  pl.BlockSpec(wukt.shape, lambda i: (0, 0, 0)),
                  pl.BlockSpec(perm.shape, lambda i: (0, 0))],
        out_specs=[pl.BlockSpec((LANES, n_heads * 256), lambda i: (i, 0)),
                   pl.BlockSpec((LANES, n_heads * LANES), lambda i: (i, 0))],
        out_shape=[jax.ShapeDtypeStruct((ns, n_heads * 256), BF16),
                   jax.ShapeDtypeStruct((ns, n_heads * LANES), BF16)],
        compiler_params=_cparams(("arbitrary",)),
        name="sample_q",
    )(qm, gkn, wukt, perm)


def _prompt_attn_kernel(qm_ref, qd_ref, km_ref, cb_ref, kd_ref, vd_ref, wuv_ref, b0_ref, b1_ref,
                        gsub_ref, lam_ref, cat_ref, m_sc, l_sc, acc_sc,
                        *, n_heads, n_kvheads, per_kv, sub_scale):
    i = pl.program_id(1)
    row = lax.broadcasted_iota(jnp.int32, (PAGE, PAGE), 0)
    col = lax.broadcasted_iota(jnp.int32, (PAGE, PAGE), 1)
    causal = col <= row

    for h in range(n_heads):
        q = qm_ref[:, h * 256:(h + 1) * 256]

        def mla_step(j, carry, masked, h=h, q=q):
            m, l, acc = carry
            r0 = pl.multiple_of(j * PAGE, PAGE)
            k = km_ref[pl.ds(r0, PAGE), h * 256:(h + 1) * 256]
            s = _dot_nt(q, k)
            if masked:
                s = jnp.where(causal, s, NEG_INF)
            m_new = jnp.maximum(m, jnp.max(s, axis=-1, keepdims=True))
            alpha = jnp.exp(m - m_new)
            p = jnp.exp(s - m_new)
            l = alpha * l + jnp.sum(p, axis=-1, keepdims=True)
            acc = alpha * acc + _dot(p.astype(BF16), cb_ref[pl.ds(r0, PAGE), :])
            return m_new, l, acc

        init = (jnp.full((PAGE, 1), NEG_INF, F32), jnp.zeros((PAGE, 1), F32),
                jnp.zeros((PAGE, cb_ref.shape[1]), F32))
        carry = lax.fori_loop(0, i, functools.partial(mla_step, masked=False), init)
        _, l, acc = mla_step(i, carry, True)
        olat = acc * (1.0 / l)
        cat_ref[:, h * LANES:(h + 1) * LANES] = _dot(olat.astype(BF16), wuv_ref[h]).astype(BF16)

    lam = lam_ref[0:1, 0:1]
    n_stack = per_kv * 2
    for g in range(n_kvheads):
        base = g * n_stack
        qs = jnp.concatenate([qd_ref[:, (base + t) * LANES:(base + t + 1) * LANES] for t in range(n_stack)],
                             axis=0)
        m_sc[...] = jnp.full(m_sc.shape, NEG_INF, F32)
        l_sc[...] = jnp.zeros(l_sc.shape, F32)
        acc_sc[...] = jnp.zeros(acc_sc.shape, F32)

        def diff_step(j, bias_ref, g=g, qs=qs):
            r0 = pl.multiple_of(j * PAGE, PAGE)
            k = kd_ref[pl.ds(r0, PAGE), g * LANES:(g + 1) * LANES]
            s = _dot_nt(qs, k)
            if bias_ref is not None:
                s = s + bias_ref[g]
            m_old = m_sc[...]
            m_new = jnp.maximum(m_old, jnp.max(s, axis=-1, keepdims=True))
            alpha = jnp.exp(m_old - m_new)
            p = jnp.exp(s - m_new)
            l_sc[...] = alpha * l_sc[...] + jnp.sum(p, axis=-1, keepdims=True)
            v = vd_ref[pl.ds(r0, PAGE), g * LANES:(g + 1) * LANES]
            acc_sc[...] = alpha * acc_sc[...] + _dot(p.astype(BF16), v)
            m_sc[...] = m_new

        def far_body(j, c):
            diff_step(j, None)
            return c

        lax.fori_loop(0, jnp.maximum(i - 1, 0), far_body, 0)

        @pl.when(i >= 1)
        def _():
            diff_step(i - 1, b1_ref)

        diff_step(i, b0_ref)
        o = acc_sc[...] * (1.0 / l_sc[...])
        for r in range(per_kv):
            o1 = o[(2 * r) * PAGE:(2 * r + 1) * PAGE]
            o2 = o[(2 * r + 1) * PAGE:(2 * r + 2) * PAGE]
            od = o1 - lam * o2
            odn = od * _rms_rows(od, LANES) * gsub_ref[...] * sub_scale
            c0 = (n_heads + g * per_kv + r) * LANES
            cat_ref[:, c0:c0 + LANES] = odn.astype(BF16)


def _prompt_attn(qm, qd, km, cb, kdb, vdb, wuv, b0, b1, gsub, lam, nf, batch, lp, dims, sub_scale):
    n_heads, n_dheads, n_kvheads = dims["n_heads"], dims["n_dheads"], dims["n_kvheads"]
    per_kv = n_dheads // n_kvheads
    nqb = lp // PAGE
    qrow = lambda w: pl.BlockSpec((PAGE, w), lambda b, i: (b * nqb + i, 0))
    kvrow = lambda w: pl.BlockSpec((lp, w), lambda b, i: (b, 0))
    full = lambda a: pl.BlockSpec(a.shape, lambda b, i: (0,) * a.ndim)
    cat_w = (n_heads + n_dheads) * LANES
    n_stack = per_kv * 2 * PAGE
    return pl.pallas_call(
        functools.partial(_prompt_attn_kernel, n_heads=n_heads, n_kvheads=n_kvheads, per_kv=per_kv,
                          sub_scale=sub_scale),
        grid=(batch, nqb),
        in_specs=[qrow(n_heads * 256), qrow(n_dheads * 256), kvrow(n_heads * 256), kvrow(cb.shape[1]),
                  kvrow(n_kvheads * LANES), kvrow(n_kvheads * LANES),
                  full(wuv), full(b0), full(b1), full(gsub), full(lam)],
        out_specs=qrow(cat_w),
        out_shape=jax.ShapeDtypeStruct((batch * lp, cat_w), BF16),
        scratch_shapes=[pltpu.VMEM((n_stack, 1), F32), pltpu.VMEM((n_stack, 1), F32),
                        pltpu.VMEM((n_stack, LANES), F32)],
        compiler_params=_cparams(("arbitrary", "arbitrary")),
        name="prompt_attn",
    )(qm, qd, km, cb, kdb, vdb, wuv, b0, b1, gsub, lam)


def _decode_kernel(pt_ref, lat_hbm, krp_hbm, dk_hbm, dv_hbm,
                   qabs_ref, lhs2_ref, qdl_ref, cnew_ref, krnew_ref, kdnew_ref, vdnew_ref,
                   wukt_ref, cs_ref, gg_ref, tlast_ref, tnew_ref, mnew_ref, lam_ref,
                   olat_ref, od_ref,
                   cbuf, rbuf, kbuf, vbuf, sem, m1, l1, a1, m2, l2, a2,
                   *, layer, n_seq, n_chunks, pages, n_heads, n_kvheads, s_len, kv_lora):
    b = pl.program_id(0)
    t_chunk = pages * PAGE
    hs = n_heads * s_len
    ds_rows = m2.shape[1]

    def page_copies(seq, chunk, slot):
        cps = []
        for p in range(pages):
            pg = pt_ref[seq, chunk * pages + p]
            cps.append(pltpu.make_async_copy(lat_hbm.at[layer, pg],
                                             cbuf.at[slot, pl.ds(p * PAGE, PAGE), :], sem.at[slot]))
            cps.append(pltpu.make_async_copy(krp_hbm.at[layer, pg],
                                             rbuf.at[slot, pl.ds(p * PAGE, PAGE), :], sem.at[slot]))
            cps.append(pltpu.make_async_copy(dk_hbm.at[layer, pg],
                                             kbuf.at[slot, pl.ds(p * 2 * PAGE, 2 * PAGE), :], sem.at[slot]))
            cps.append(pltpu.make_async_copy(dv_hbm.at[layer, pg],
                                             vbuf.at[slot, pl.ds(p * 2 * PAGE, 2 * PAGE), :], sem.at[slot]))
        return cps

    def start_chunk(seq, chunk, slot):
        for cp in page_copies(seq, chunk, slot):
            cp.start()

    def wait_chunk(seq, chunk, slot):
        for cp in page_copies(seq, chunk, slot):
            cp.wait()

    @pl.when(b == 0)
    def _():
        start_chunk(0, 0, 0)

    m1[...] = jnp.full(m1.shape, NEG_INF, F32)
    l1[...] = jnp.zeros(l1.shape, F32)
    a1[...] = jnp.zeros(a1.shape, F32)
    m2[...] = jnp.full(m2.shape, NEG_INF, F32)
    l2[...] = jnp.zeros(l2.shape, F32)
    a2[...] = jnp.zeros(a2.shape, F32)

    qabs = qabs_ref[0]
    lhs2 = lhs2_ref[0]
    wukt = wukt_ref[...]
    gg = gg_ref[...]

    def attend(c32, kr32, cs, kd_g, vd_g, mla_mask, diff_bias):
        t = c32.shape[0]
        cbk = c32.astype(BF16)
        kt = _dot_nt(wukt, cbk)
        sn = _dot_nt(qabs, cbk)
        xx = jnp.concatenate([kr32, kr32], axis=1)
        f2 = jnp.concatenate([xx * gg * cs, xx * xx], axis=1).astype(BF16)
        s2 = _dot_nt(lhs2, f2)
        krsq = s2[hs:hs + 8]
        sq = kt * kt
        rows = []
        for h in range(n_heads):
            nsq = jnp.sum(sq[h * NOPE:(h + 1) * NOPE], axis=0, keepdims=True)
            rinv = lax.rsqrt((nsq + krsq) * (1.0 / QK) + EPS)
            rows.append((sn[h * s_len:(h + 1) * s_len] + s2[h * s_len:(h + 1) * s_len]) * rinv)
        s = jnp.concatenate(rows, axis=0)
        if mla_mask is not None:
            s = s + mla_mask
        m_old = m1[...]
        m_new = jnp.maximum(m_old, jnp.max(s, axis=-1, keepdims=True))
        alpha = jnp.exp(m_old - m_new)
        p = jnp.exp(s - m_new)
        psum = p[:, 0:LANES]
        for k in range(1, t // LANES):
            psum = psum + p[:, k * LANES:(k + 1) * LANES]
        l1[...] = alpha * l1[...] + psum
        a1[...] = alpha * a1[...] + _dot(p.astype(BF16), cbk)
        m1[...] = m_new
        for g in range(n_kvheads):
            kg = kd_g[g].astype(BF16)
            vg = vd_g[g].astype(BF16)
            sg = _dot_nt(qdl_ref[0, g], kg)
            if diff_bias is not None:
                sg = sg + diff_bias[g]
            mo = m2[g]
            mn = jnp.maximum(mo, jnp.max(sg, axis=-1, keepdims=True))
            al = jnp.exp(mo - mn)
            pg = jnp.exp(sg - mn)
            ps = pg[:, 0:LANES]
            for k in range(1, t // LANES):
                ps = ps + pg[:, k * LANES:(k + 1) * LANES]
            l2[g] = al * l2[g] + ps
            a2[g] = al * a2[g] + _dot(pg.astype(BF16), vg)
            m2[g] = mn

    def chunk_inputs(slot, chunk):
        c32 = cbuf[slot]
        kr32 = rbuf[slot]
        cs = cs_ref[pl.ds(pl.multiple_of(chunk * t_chunk, t_chunk), t_chunk), :]
        kd_g = [kbuf[slot, pl.ds(g, t_chunk, stride=n_kvheads), :] for g in range(n_kvheads)]
        vd_g = [vbuf[slot, pl.ds(g, t_chunk, stride=n_kvheads), :] for g in range(n_kvheads)]
        return c32, kr32, cs, kd_g, vd_g

    def chunk_step(chunk, with_bias):
        gidx = b * n_chunks + chunk
        slot = lax.rem(gidx, 2)
        wait_chunk(b, chunk, slot)
        nxt = chunk + 1

        @pl.when(nxt < n_chunks)
        def _():
            start_chunk(b, nxt, 1 - slot)

        @pl.when(jnp.logical_and(nxt == n_chunks, b + 1 < n_seq))
        def _():
            start_chunk(b + 1, 0, 1 - slot)

        c32, kr32, cs, kd_g, vd_g = chunk_inputs(slot, chunk)
        bias = None
        if with_bias:
            zeros = jnp.zeros((ds_rows, t_chunk - PAGE), F32)
            bias = [jnp.concatenate([zeros, tlast_ref[g]], axis=1) if t_chunk > PAGE else tlast_ref[g]
                    for g in range(n_kvheads)]
        attend(c32, kr32, cs, kd_g, vd_g, None, bias)

    def body(chunk, c):
        chunk_step(chunk, False)
        return c

    lax.fori_loop(0, n_chunks - 1, body, 0)
    chunk_step(n_chunks - 1, True)

    pad = PAGE - s_len
    padr = lambda x: jnp.concatenate([x, jnp.zeros((pad, x.shape[1]), x.dtype)], axis=0)
    c_new = padr(cnew_ref[0])
    kr_new = padr(krnew_ref[0])
    kd_new = padr(kdnew_ref[0])
    vd_new = padr(vdnew_ref[0])
    cs_new = cs_ref[pl.ds(n_chunks * t_chunk, PAGE), :]
    kd_g = [kd_new[:, g * LANES:(g + 1) * LANES] for g in range(n_kvheads)]
    vd_g = [vd_new[:, g * LANES:(g + 1) * LANES] for g in range(n_kvheads)]
    attend(c_new, kr_new, cs_new, kd_g, vd_g, mnew_ref[...], [tnew_ref[g] for g in range(n_kvheads)])

    olat_ref[0] = a1[...] * (1.0 / jnp.sum(l1[...], axis=-1, keepdims=True))
    lam = lam_ref[0:1, 0:1]
    half = ds_rows // 2
    for g in range(n_kvheads):
        o = a2[g] * (1.0 / jnp.sum(l2[g], axis=-1, keepdims=True))
        od_ref[0, g] = o[:half] - lam * o[half:]


def _decode_attn(page_table, lat, krp, dkp, dvp, qabs, lhs2, qdl, c_new, kr_new, kd_new, vd_new,
                 wukt, cs, gg, tlast, tnew, mnew, lam, layer, dims, pages):
    n_seq, n_pages = page_table.shape
    n_heads, n_kvheads = dims["n_heads"], dims["n_kvheads"]
    s_len = c_new.shape[1]
    kv_lora = c_new.shape[2]
    n_chunks = n_pages // pages
    t_chunk = pages * PAGE
    hs = n_heads * s_len
    ds_rows = qdl.shape[2]
    seq3 = lambda a: pl.BlockSpec((1,) + a.shape[1:], lambda b, pt: (b,) + (0,) * (a.ndim - 1))
    full = lambda a: pl.BlockSpec(a.shape, lambda b, pt: (0,) * a.ndim)
    hbm = pl.BlockSpec(memory_space=pl.ANY)
    grid_spec = pltpu.PrefetchScalarGridSpec(
        num_scalar_prefetch=1,
        grid=(n_seq,),
        in_specs=[hbm, hbm, hbm, hbm,
                  seq3(qabs), seq3(lhs2), seq3(qdl), seq3(c_new), seq3(kr_new), seq3(kd_new), seq3(vd_new),
                  full(wukt), full(cs), full(gg), full(tlast), full(tnew), full(mnew), full(lam)],
        out_specs=[pl.BlockSpec((1, hs, kv_lora), lambda b, pt: (b, 0, 0)),
                   pl.BlockSpec((1, n_kvheads, ds_rows // 2, LANES), lambda b, pt: (b, 0, 0, 0))],
        scratch_shapes=[pltpu.VMEM((2, t_chunk, kv_lora), F32),
                        pltpu.VMEM((2, t_chunk, ROPE), F32),
                        pltpu.VMEM((2, n_kvheads * t_chunk, LANES), F32),
                        pltpu.VMEM((2, n_kvheads * t_chunk, LANES), F32),
                        pltpu.SemaphoreType.DMA((2,)),
                        pltpu.VMEM((hs, 1), F32), pltpu.VMEM((hs, LANES), F32), pltpu.VMEM((hs, kv_lora), F32),
                        pltpu.VMEM((n_kvheads, ds_rows, 1), F32), pltpu.VMEM((n_kvheads, ds_rows, LANES), F32),
                        pltpu.VMEM((n_kvheads, ds_rows, LANES), F32)],
    )
    return pl.pallas_call(
        functools.partial(_decode_kernel, layer=layer, n_seq=n_seq, n_chunks=n_chunks, pages=pages,
                          n_heads=n_heads, n_kvheads=n_kvheads, s_len=s_len, kv_lora=kv_lora),
        grid_spec=grid_spec,
        out_shape=[jax.ShapeDtypeStruct((n_seq, hs, kv_lora), F32),
                   jax.ShapeDtypeStruct((n_seq, n_kvheads, ds_rows // 2, LANES), F32)],
        compiler_params=_cparams(("arbitrary",)),
        name="decode_attn",
    )(page_table, lat, krp, dkp, dvp, qabs, lhs2, qdl, c_new, kr_new, kd_new, vd_new,
      wukt, cs, gg, tlast, tnew, mnew, lam)


def _sample_post_kernel(olat_ref, od_ref, wuv_ref, gsub_ref, cat_ref, *, n_heads, n_dheads, kv_lora, sub_scale):
    for h in range(n_heads):
        o = olat_ref[:, h * kv_lora:(h + 1) * kv_lora].astype(BF16)
        cat_ref[:, h * LANES:(h + 1) * LANES] = _dot(o, wuv_ref[h]).astype(BF16)
    for h in range(n_dheads):
        od = od_ref[:, h * LANES:(h + 1) * LANES]
        odn = od * _rms_rows(od, LANES) * gsub_ref[...] * sub_scale
        cat_ref[:, (n_heads + h) * LANES:(n_heads + h + 1) * LANES] = odn.astype(BF16)


def _sample_post(olat, od, wuv, gsub, n_rows, dims, sub_scale):
    ns = olat.shape[0]
    n_heads, n_dheads = dims["n_heads"], dims["n_dheads"]
    kv_lora = dims["kv_lora"]
    cat_w = (n_heads + n_dheads) * LANES
    return pl.pallas_call(
        functools.partial(_sample_post_kernel, n_heads=n_heads, n_dheads=n_dheads, kv_lora=kv_lora,
                          sub_scale=sub_scale),
        grid=(ns // LANES,),
        in_specs=[pl.BlockSpec((LANES, olat.shape[1]), lambda i: (i, 0)),
                  pl.BlockSpec((LANES, od.shape[1]), lambda i: (i, 0)),
                  pl.BlockSpec(wuv.shape, lambda i: (0, 0, 0)),
                  pl.BlockSpec(gsub.shape, lambda i: (0, 0))],
        out_specs=pl.BlockSpec((LANES, cat_w), lambda i: (i, 0)),
        out_shape=jax.ShapeDtypeStruct((n_rows, cat_w), BF16),
        compiler_params=_cparams(("arbitrary",)),
        name="sample_post",
    )(olat, od, wuv, gsub)


def _merge_kernel(catp_ref, cats_ref, h_ref, wo_ref, gffn_ref, wrh_ref, wrl_ref, br_ref,
                  h1_ref, hn_ref, ti_ref, tw_ref, *, d_model, n_experts, n_pblocks):
    cat = jnp.where(pl.program_id(0) < n_pblocks, catp_ref[...], cats_ref[...])
    h1 = h_ref[...] + _dot(cat, wo_ref[...])
    h1_ref[...] = h1
    hn = h1 * _rms_rows(h1, d_model) * gffn_ref[...]
    hn_ref[...] = hn
    hi, lo = _split_bf16(hn)
    logits = _dot(hi, wrh_ref[...]) + _dot(lo, wrh_ref[...]) + _dot(hi, wrl_ref[...]) + br_ref[...]
    lane = lax.broadcasted_iota(jnp.int32, logits.shape, 1)
    lane_f = lane.astype(F32)
    x = jnp.where(lane < n_experts, logits, NEG_INF)
    vals, idxs = [], []
    for _ in range(TOP_K):
        mk = jnp.max(x, axis=-1, keepdims=True)
        ik = jnp.min(jnp.where(x == mk, lane_f, float(LANES)), axis=-1, keepdims=True).astype(jnp.int32)
        vals.append(mk)
        idxs.append(ik)
        x = jnp.where(lane == ik, NEG_INF, x)
    es = [jnp.exp(v - vals[0]) for v in vals]
    den = es[0]
    for e in es[1:]:
        den = den + e
    inv = 1.0 / den
    ti = jnp.zeros(logits.shape, jnp.int32)
    tw = jnp.zeros(logits.shape, F32)
    for k in range(TOP_K):
        ti = jnp.where(lane == k, idxs[k], ti)
        tw = jnp.where(lane == k, es[k] * inv, tw)
    ti_ref[...] = ti
    tw_ref[...] = tw


def _merge(cat_p, cat_s, h_flat, wo, gffn, wrh, wrl, br, n_experts):
    nf, d_model = h_flat.shape
    n_pblocks = cat_p.shape[0] // TM
    n_sblocks = cat_s.shape[0] // TM
    row = lambda w: pl.BlockSpec((TM, w), lambda i: (i, 0))
    full = lambda a: pl.BlockSpec(a.shape, lambda i: (0,) * a.ndim)
    return pl.pallas_call(
        functools.partial(_merge_kernel, d_model=d_model, n_experts=n_experts, n_pblocks=n_pblocks),
        grid=(nf // TM,),
        in_specs=[pl.BlockSpec((TM, cat_p.shape[1]), lambda i: (jnp.minimum(i, n_pblocks - 1), 0)),
                  pl.BlockSpec((TM, cat_s.shape[1]),
                               lambda i: (jnp.clip(i - n_pblocks, 0, n_sblocks - 1), 0)),
                  row(d_model), full(wo), full(gffn), full(wrh), full(wrl), full(br)],
        out_specs=[row(d_model), row(d_model), row(LANES), row(LANES)],
        out_shape=[jax.ShapeDtypeStruct((nf, d_model), F32), jax.ShapeDtypeStruct((nf, d_model), F32),
                   jax.ShapeDtypeStruct((nf, LANES), jnp.int32), jax.ShapeDtypeStruct((nf, LANES), F32)],
        compiler_params=_cparams(("arbitrary",)),
        name="merge",
    )(cat_p, cat_s, h_flat, wo, gffn, wrh, wrl, br)


def _expert_kernel(ie_ref, in_ref, tok_ref, hn_hbm, wg_ref, wu_ref, bg_ref, bu_ref, wd_ref, bd_ref, out_hbm,
                   tok_smem, xs, acc, gsem, tsem, osem, *, n_fc):
    w = pl.program_id(0)
    f = pl.program_id(1)
    n = in_ref[w]
    n_sub = (n + RS - 1) // RS

    def out_copy(item):
        return pltpu.make_async_copy(acc, out_hbm.at[item], osem)

    @pl.when(jnp.logical_and(f == 0, n > 0))
    def _():
        cp = pltpu.make_async_copy(tok_ref.at[0, 0], tok_smem, tsem)
        cp.start()
        cp.wait()

        def issue(r, c):
            pltpu.make_async_copy(hn_hbm.at[pl.ds(tok_smem[r], 1), :], xs.at[pl.ds(r, 1), :], gsem).start()
            return c

        lax.fori_loop(0, n_sub * RS, issue, 0)

        def drain(r, c):
            pltpu.make_async_copy(hn_hbm.at[pl.ds(0, 1), :], xs.at[pl.ds(r, 1), :], gsem).wait()
            return c

        lax.fori_loop(0, n_sub * RS, drain, 0)
        acc[...] = jnp.zeros(acc.shape, F32)

    @pl.when(n > 0)
    def _():
        wg = wg_ref[0, 0].astype(BF16)
        wu = wu_ref[0, 0].astype(BF16)
        wd = wd_ref[0, 0].astype(BF16)
        bg = bg_ref[0, 0]
        bu = bu_ref[0, 0]

        def sub(t, c):
            r0 = pl.multiple_of(t * RS, RS)
            x = xs[pl.ds(r0, RS), :].astype(BF16)
            gate = jnp.minimum(_dot(x, wg) + bg, SWIGLU_LIMIT)
            up = jnp.clip(_dot(x, wu) + bu, -SWIGLU_LIMIT, SWIGLU_LIMIT)
            act = (up + 1.0) * gate * jax.nn.sigmoid(SWIGLU_ALPHA * gate)
            acc[pl.ds(r0, RS), :] += _dot(act.astype(BF16), wd)
            return c

        lax.fori_loop(0, n_sub, sub, 0)

    @pl.when(jnp.logical_and(f == n_fc - 1, n > 0))
    def _():
        acc[...] = acc[...] + bd_ref[...]
        cp = out_copy(w)
        cp.start()
        cp.wait()


def _experts(item_e, item_n, tok_items, hn, w_gate_up, b_gate_up, w_down, b_down, layer):
    n_items = item_e.shape[0]
    d_model = hn.shape[1]
    d_ff = w_down.shape[2]
    n_fc = d_ff // FC
    bgu = b_gate_up.reshape(b_gate_up.shape[0], b_gate_up.shape[1], 1, 2 * d_ff)
    bd = b_down.reshape(b_down.shape[0], b_down.shape[1], 1, d_model)
    grid_spec = pltpu.PrefetchScalarGridSpec(
        num_scalar_prefetch=2,
        grid=(n_items, n_fc),
        in_specs=[pl.BlockSpec((1, 1, CAP), lambda w, f, ie, inn: (w, 0, 0)),
                  pl.BlockSpec(memory_space=pl.ANY),
                  pl.BlockSpec((1, 1, d_model, FC), lambda w, f, ie, inn: (layer, ie[w], 0, f)),
                  pl.BlockSpec((1, 1, d_model, FC), lambda w, f, ie, inn: (layer, ie[w], 0, n_fc + f)),
                  pl.BlockSpec((1, 1, 1, FC), lambda w, f, ie, inn: (layer, ie[w], 0, f)),
                  pl.BlockSpec((1, 1, 1, FC), lambda w, f, ie, inn: (layer, ie[w], 0, n_fc + f)),
                  pl.BlockSpec((1, 1, FC, d_model), lambda w, f, ie, inn: (layer, ie[w], f, 0)),
                  pl.BlockSpec((None, None, 1, d_model), lambda w, f, ie, inn: (layer, ie[w], 0, 0))],
        out_specs=pl.BlockSpec(memory_space=pl.ANY),
        scratch_shapes=[pltpu.SMEM((CAP,), jnp.int32),
                        pltpu.VMEM((CAP, d_model), F32),
                        pltpu.VMEM((CAP, d_model), F32),
                        pltpu.SemaphoreType.DMA(()), pltpu.SemaphoreType.DMA(()), pltpu.SemaphoreType.DMA(())],
    )
    return pl.pallas_call(
        functools.partial(_expert_kernel, n_fc=n_fc),
        grid_spec=grid_spec,
        out_shape=jax.ShapeDtypeStruct((n_items, CAP, d_model), F32),
        compiler_params=_cparams(("arbitrary", "arbitrary")),
        name="experts",
    )(item_e, item_n, tok_items, hn, w_gate_up, w_gate_up, bgu, bgu, w_down, bd)


def _combine_kernel(slot_ref, h1_ref, tw_ref, eo_hbm, out_ref, slot_smem, buf, ssem, gsem):
    cp = pltpu.make_async_copy(slot_ref.at[0, 0], slot_smem, ssem)
    cp.start()
    cp.wait()

    def issue(r, c):
        for k in range(TOP_K):
            pltpu.make_async_copy(eo_hbm.at[pl.ds(slot_smem[r * TOP_K + k], 1), :],
                                  buf.at[k, pl.ds(r, 1), :], gsem).start()
        return c

    lax.fori_loop(0, TC, issue, 0)

    def drain(r, c):
        for k in range(TOP_K):
            pltpu.make_async_copy(eo_hbm.at[pl.ds(0, 1), :], buf.at[k, pl.ds(r, 1), :], gsem).wait()
        return c

    lax.fori_loop(0, TC, drain, 0)
    tw = tw_ref[...]
    out = h1_ref[...]
    for k in range(TOP_K):
        out = out + tw[:, k:k + 1] * buf[k]
    out_ref[...] = out


def _combine(slots, h1, tw, eo):
    nf, d_model = h1.shape
    row = lambda w: pl.BlockSpec((TC, w), lambda i: (i, 0))
    return pl.pallas_call(
        _combine_kernel,
        grid=(nf // TC,),
        in_specs=[pl.BlockSpec((1, 1, TC * TOP_K), lambda i: (i, 0, 0)),
                  row(d_model), row(LANES), pl.BlockSpec(memory_space=pl.ANY)],
        out_specs=row(d_model),
        out_shape=jax.ShapeDtypeStruct((nf, d_model), F32),
        scratch_shapes=[pltpu.SMEM((TC * TOP_K,), jnp.int32),
                        pltpu.VMEM((TOP_K, TC, d_model), F32),
                        pltpu.SemaphoreType.DMA(()), pltpu.SemaphoreType.DMA(())],
        compiler_params=_cparams(("arbitrary",)),
        name="combine",
    )(slots, h1, tw, eo)


def _route(ti, valid, n_experts, n_items):
    nf = ti.shape[0]
    e_flat = jnp.where(valid[:, None], ti[:, :TOP_K], n_experts).reshape(-1)
    order = jnp.argsort(e_flat, stable=True).astype(jnp.int32)
    e_sorted = e_flat[order]
    counts = jnp.sum(e_flat[:, None] == jnp.arange(n_experts, dtype=jnp.int32)[None, :], axis=0).astype(jnp.int32)
    seg_start = jnp.cumsum(counts) - counts
    items_per = (counts + CAP - 1) // CAP
    item_cum = jnp.cumsum(items_per)
    item_first = item_cum - items_per
    total = item_cum[-1]
    w = jnp.arange(n_items, dtype=jnp.int32)
    e_of_w = jnp.minimum(jnp.searchsorted(item_cum, w, side="right"), n_experts - 1).astype(jnp.int32)
    j = w - item_first[e_of_w]
    start = seg_start[e_of_w] + j * CAP
    n = jnp.clip(counts[e_of_w] - j * CAP, 0, CAP)
    active = w < total
    e_last = e_of_w[jnp.maximum(total - 1, 0)]
    item_e = jnp.where(active, e_of_w, e_last).astype(jnp.int32)
    item_n = jnp.where(active, n, 0).astype(jnp.int32)
    item_start = jnp.where(active, start, 0)
    r = jnp.arange(CAP, dtype=jnp.int32)
    pos = jnp.minimum(item_start[:, None] + r[None, :], nf * TOP_K - 1)
    tok_items = jnp.where(r[None, :] < item_n[:, None], order[pos] // TOP_K, 0).astype(jnp.int32)
    p = jnp.arange(nf * TOP_K, dtype=jnp.int32)
    es = jnp.minimum(e_sorted, n_experts - 1)
    rel = p - seg_start[es]
    slot_sorted = (item_first[es] + rel // CAP) * CAP + rel % CAP
    slot_sorted = jnp.where(e_sorted < n_experts, slot_sorted, 0).astype(jnp.int32)
    slot_flat = jnp.zeros((nf * TOP_K,), jnp.int32).at[order].set(slot_sorted)
    return item_e, item_n, tok_items.reshape(n_items, 1, CAP), slot_flat.reshape(nf // TC, 1, TC * TOP_K)


def _rope_tables(pos):
    half = ROPE // 2
    inv = jnp.power(ROPE_THETA, -jnp.arange(half, dtype=F32) / half)
    ang = pos.astype(F32)[:, None] * inv[None, :]
    return jnp.cos(ang), jnp.sin(ang)


def kernel(x_prompt, x_sample, cache_mla_latent, cache_mla_krope, cache_diff_k, cache_diff_v, page_table,
           meta_tokens, rel_bias, g_attn, w_in, g_qa, w_qb, g_kv, w_kvb, g_mla_q, g_mla_k, g_diff_q, g_diff_k,
           lambda_q1, lambda_k1, lambda_q2, lambda_k2, g_subln, w_o, g_ffn, w_router, b_router,
           w_gate_up, b_gate_up, w_down, b_down):
    batch, seq, d_model = x_prompt.shape
    n_seq, s_len = x_sample.shape[0], x_sample.shape[1]
    depth = g_attn.shape[0]
    q_lora = g_qa.shape[1]
    kv_lora = g_kv.shape[1]
    n_heads = w_qb.shape[2]
    n_dheads = rel_bias.shape[1]
    n_kvheads = cache_diff_k.shape[3]
    per_kv = n_dheads // n_kvheads
    n_experts = w_router.shape[2]
    n_pool = cache_mla_latent.shape[1]
    n_pages = page_table.shape[1]
    past = n_pages * PAGE
    assert depth == 1 and w_qb.shape[3] == QK and cache_mla_krope.shape[3] == ROPE
    assert cache_diff_k.shape[4] == 2 * HALF and cache_diff_v.shape[4] == LANES and w_kvb.shape[3] == 2 * NOPE
    assert s_len == 8 and n_heads == 8 and n_experts <= LANES
    dims = dict(n_heads=n_heads, n_dheads=n_dheads, n_kvheads=n_kvheads, q_lora=q_lora, kv_lora=kv_lora)
    layer = 0
    lam_init = 0.8 - 0.6 * math.exp(-0.3 * layer)
    sub_scale = 1.0 - lam_init
    mla_scale = QK ** -0.5
    diff_scale = HALF ** -0.5

    l_tot = seq + N_META
    lp = -(-l_tot // PAGE) * PAGE
    n_p = batch * lp
    n_s = n_seq * s_len
    assert n_s % LANES == 0 and n_p % TM == 0
    nf = -(-(n_p + n_s) // TM) * TM
    meta = jnp.broadcast_to(meta_tokens.astype(F32)[None], (batch, N_META, d_model))
    h_p = jnp.concatenate([meta, x_prompt, jnp.zeros((batch, lp - l_tot, d_model), F32)], axis=1)
    h_flat = jnp.concatenate([h_p.reshape(n_p, d_model), x_sample.reshape(n_s, d_model),
                              jnp.zeros((nf - n_p - n_s, d_model), F32)], axis=0)
    pos = jnp.concatenate([jnp.tile(jnp.arange(lp, dtype=jnp.int32), batch),
                           jnp.tile(past + jnp.arange(s_len, dtype=jnp.int32), n_seq),
                           jnp.zeros((nf - n_p - n_s,), jnp.int32)])
    cos, sin = _rope_tables(pos)
    z = jnp.zeros_like(cos)
    ct = jnp.concatenate([cos, cos, z, z], axis=1)
    st = jnp.concatenate([-sin, sin, z, z], axis=1)
    row_idx = jnp.arange(nf, dtype=jnp.int32)
    valid = jnp.where(row_idx < n_p, (row_idx % lp) < l_tot, row_idx < n_p + n_s)

    w_in_l = w_in[layer]
    o = np.cumsum([0, q_lora, kv_lora, ROPE, n_dheads * 2 * HALF, n_kvheads * 2 * HALF, n_kvheads * LANES])
    w_in_p = jnp.concatenate([w_in_l[:, o[0]:o[2]], w_in_l[:, o[2]:o[3]], w_in_l[:, o[2]:o[3]],
                              w_in_l[:, o[3]:o[6]]], axis=1).astype(BF16)
    wq = w_qb[layer]
    w_q = jnp.concatenate([wq, wq[:, :, NOPE:]], axis=2).reshape(q_lora, n_heads * 256).astype(BF16)
    gq = g_mla_q[layer] * mla_scale
    g_q = jnp.tile(jnp.concatenate([gq, gq[NOPE:]]), n_heads)[None]
    gk = g_mla_k[layer]
    w_uk3 = w_kvb[layer][:, :, :NOPE]
    w_uv3 = w_kvb[layer][:, :, NOPE:]
    prm = dict(
        g_attn=g_attn[layer][None], w_in=w_in_p, g_qa=g_qa[layer][None], w_q=w_q, g_q=g_q,
        g_kv=g_kv[layer][None], w_uk=w_uk3.reshape(kv_lora, n_heads * NOPE).astype(BF16),
        g_kn=jnp.tile(gk[:NOPE], n_heads)[None], g_kr=jnp.concatenate([gk[NOPE:], gk[NOPE:]])[None],
        g_dq=jnp.tile(g_diff_q[layer] * diff_scale, 2 * n_dheads)[None],
        g_dk=jnp.tile(g_diff_k[layer], 2 * n_kvheads)[None])
    prm["segq"], prm["expq"] = _seg_mats(n_heads * 256, 256, n_heads, skip=lambda i: i % 256 >= QK)
    prm["segk"], prm["expk"] = _seg_mats(n_heads * NOPE, NOPE, n_heads)
    prm["segdq"], prm["expdq"] = _seg_mats(n_dheads * 2 * HALF, HALF, 2 * n_dheads)
    prm["segdk"], prm["expdk"] = _seg_mats(n_kvheads * 2 * HALF, HALF, 2 * n_kvheads)

    c_f, kr_f, kd_f, vd_f, qm, km, cb, qd, kdb, vdb = _project(h_flat, ct, st, prm, dims)

    si = np.arange(PAGE)[:, None]
    ti_ = np.arange(PAGE)[None, :]
    bk0 = np.where(si >= ti_, _bucket_np(si - ti_), -1)
    bk1 = _bucket_np(PAGE + si - ti_)
    drow = np.arange(2 * s_len * per_kv)
    d_s = (drow // per_kv) % s_len
    d_r = drow % per_kv
    n_drow = drow.shape[0]
    assert n_drow <= PAGE
    bk_last = np.zeros((PAGE, PAGE), np.int32)
    bk_last[:n_drow] = _bucket_np(PAGE + d_s[:, None] - ti_)
    bk_new = np.full((PAGE, PAGE), -1, np.int32)
    bk_new[:n_drow] = np.where((ti_ <= d_s[:, None]) & (ti_ < s_len), _bucket_np(d_s[:, None] - ti_), -1)
    bks = [bk0] * n_dheads + [bk1] * n_dheads + [bk_last] * n_kvheads + [bk_new] * n_kvheads
    rb_t = rel_bias.astype(F32).T
    head_rows = [jnp.broadcast_to(rb_t[h][None], (PAGE, N_BUCKETS)) for h in range(n_dheads)]
    dec_rows = []
    for g in range(n_kvheads):
        sel = np.zeros((PAGE,), np.int32)
        sel[:n_drow] = g * per_kv + d_r
        dec_rows.append(rb_t[jnp.asarray(sel)])
    rts = head_rows + head_rows + dec_rows + dec_rows
    tiles, lam = _bias_prep(jnp.asarray(np.stack(bks)), jnp.stack(rts), lambda_q1[layer][None],
                            lambda_k1[layer][None], lambda_q2[layer][None], lambda_k2[layer][None], lam_init)

    def stack_bias(t):
        t = t.reshape(n_kvheads, per_kv, 1, PAGE, PAGE)
        return jnp.broadcast_to(t, (n_kvheads, per_kv, 2, PAGE, PAGE)).reshape(n_kvheads, per_kv * 2 * PAGE, PAGE)

    b0 = stack_bias(tiles[:n_dheads])
    b1 = stack_bias(tiles[n_dheads:2 * n_dheads])
    tlast = tiles[2 * n_dheads:2 * n_dheads + n_kvheads, :n_drow]
    tnew = tiles[2 * n_dheads + n_kvheads:, :n_drow]

    wuv = jnp.transpose(w_uv3, (1, 0, 2)).astype(BF16)
    gsub = g_subln[layer][None]
    cat = _prompt_attn(qm, qd, km, cb, kdb, vdb, wuv, b0, b1, gsub, lam, nf, batch, lp, dims, sub_scale)

    wukt3 = jnp.transpose(w_uk3, (1, 2, 0)).astype(BF16)
    perm = np.zeros((LANES, LANES), np.float32)
    for i in range(32):
        perm[i, i] = 1.0
        perm[i, 32 + i] = -1.0
        perm[32 + i, 64 + i] = 1.0
        perm[32 + i, 96 + i] = 1.0
    qabs, qf = _sample_q(qm, prm["g_kn"][:, :NOPE], wukt3, jnp.asarray(perm, BF16), n_p, n_s, n_heads)
    hs = n_heads * s_len
    to_hs = lambda a, w: a.reshape(n_seq, s_len, n_heads, w).transpose(0, 2, 1, 3).reshape(n_seq, hs, w)
    qabs_b = to_hs(qabs, kv_lora)
    qf_b = to_hs(qf, LANES)
    krsq_rows = np.zeros((8, 2 * LANES), np.float32)
    krsq_rows[:, LANES:LANES + ROPE] = 1.0
    lhs2 = jnp.concatenate([jnp.concatenate([qf_b, jnp.zeros_like(qf_b)], axis=2),
                            jnp.broadcast_to(jnp.asarray(krsq_rows, BF16)[None], (n_seq, 8, 2 * LANES))], axis=1)
    qd_s = qd[n_p:n_p + n_s].reshape(n_seq, s_len, n_kvheads, per_kv, 2, LANES)
    qdl = qd_s.transpose(0, 2, 4, 1, 3, 5).reshape(n_seq, n_kvheads, n_drow, LANES)
    seq_rows = lambda a: a[n_p:n_p + n_s].reshape(n_seq, s_len, a.shape[1])
    kpos = jnp.arange(past + PAGE, dtype=jnp.int32)
    kc, ks = _rope_tables(kpos)
    cs = jnp.concatenate([kc, ks, ks, kc], axis=1)
    gkr = gk[NOPE:]
    gg = jnp.concatenate([gkr, gkr])[None]
    j_ = np.arange(LANES)[None, :]
    s_of_row = (np.arange(hs) % s_len)[:, None]
    mnew = jnp.asarray(np.where((j_ <= s_of_row) & (j_ < s_len), 0.0, NEG_INF).astype(np.float32))
    dkp = cache_diff_k.reshape(depth, n_pool, PAGE * n_kvheads, 2 * HALF)
    dvp = cache_diff_v.reshape(depth, n_pool, PAGE * n_kvheads, LANES)
    pages = 4 if n_pages % 4 == 0 else 1
    olat, od = _decode_attn(page_table, cache_mla_latent, cache_mla_krope, dkp, dvp, qabs_b, lhs2, qdl,
                            seq_rows(c_f), seq_rows(kr_f), seq_rows(kd_f), seq_rows(vd_f),
                            wukt3.reshape(n_heads * NOPE, kv_lora), cs, gg, tlast, tnew, mnew, lam,
                            layer, dims, pages)
    olat_t = olat.reshape(n_seq, n_heads, s_len, kv_lora).transpose(0, 2, 1, 3).reshape(n_s, n_heads * kv_lora)
    od_t = od.reshape(n_seq, n_kvheads, s_len, per_kv, LANES).transpose(0, 2, 1, 3, 4).reshape(n_s, n_dheads * LANES)
    cat_s = _sample_post(olat_t, od_t, wuv, gsub, nf - n_p, dims, sub_scale)

    wr = jnp.pad(w_router[layer].astype(F32), ((0, 0), (0, LANES - n_experts)))
    wrh, wrl = _split_bf16(wr)
    br = jnp.pad(b_router[layer].astype(F32), (0, LANES - n_experts))[None]
    h1, hn, ti, tw = _merge(cat, cat_s, h_flat, w_o[layer].astype(BF16), g_ffn[layer][None], wrh, wrl, br,
                            n_experts)

    n_valid = batch * l_tot + n_s
    n_items = n_experts + -(-(n_valid * TOP_K) // CAP)
    item_e, item_n, tok_items, slots = _route(ti, valid, n_experts, n_items)
    eo = _experts(item_e, item_n, tok_items, hn, w_gate_up, b_gate_up, w_down, b_down, layer)
    out = _combine(slots, h1, tw, eo.reshape(n_items * CAP, d_model))

    y_prompt = out[:n_p].reshape(batch, lp, d_model)[:, N_META:l_tot]
    y_sample = out[n_p:n_p + n_s].reshape(n_seq, s_len, d_model)
    pr = lambda a: a[:n_p].reshape(batch, lp, a.shape[1])[:, :l_tot][None]
    sr = lambda a: a[n_p:n_p + n_s].reshape(n_seq, s_len, a.shape[1])[None]
    kv5 = lambda a: a.reshape(a.shape[:3] + (n_kvheads, a.shape[3] // n_kvheads))
    return (y_prompt, y_sample, pr(c_f), pr(kr_f), kv5(pr(kd_f)), kv5(pr(vd_f)),
            sr(c_f), sr(kr_f), kv5(sr(kd_f)), kv5(sr(vd_f)))
```

```python
import functools
import math

import numpy as np
import jax
import jax.numpy as jnp
from jax import lax
from jax.experimental import pallas as pl
from jax.experimental.pallas import tpu as pltpu

F32 = jnp.float32
BF16 = jnp.bfloat16
NEG_INF = float("-inf")

N_META = 16
ROPE_THETA = 10000.0
N_BUCKETS = 32
MAX_DISTANCE = 128
TOP_K = 4
SWIGLU_LIMIT = 7.0
SWIGLU_ALPHA = 1.702
EPS = 1e-6
PAGE = 128
NOPE = 128
ROPE = 64
QK = NOPE + ROPE
HALF = 64

LANES = 128
TM = 256
VMEM_LIMIT = 56 * 1024 * 1024
DECODE_PAGES = 16

CAP = 2048
RS = 256
FC = 256
TC = 128
TOK_ALIGN = 1024
TOK_WIN = (CAP + TOK_ALIGN) // LANES

NT_DIMS = (((1,), (1,)), ((), ()))


def _cparams(sem):
    return pltpu.CompilerParams(dimension_semantics=sem, vmem_limit_bytes=VMEM_LIMIT)


def _dot(a, b):
    return jnp.dot(a, b, preferred_element_type=F32)


def _dot_nt(a, b):
    return lax.dot_general(a, b, NT_DIMS, preferred_element_type=F32)


def _split_bf16(x):
    hi = x.astype(BF16)
    lo = (x - hi.astype(F32)).astype(BF16)
    return hi, lo


def _seg_rsqrt(sq, seg_ref, exp_ref, width, extra=None):
    ssq = _dot(sq.astype(BF16), seg_ref[...])
    if extra is not None:
        ssq = ssq + extra
    rs = lax.rsqrt(ssq * (1.0 / width) + EPS)
    hi, lo = _split_bf16(rs)
    return _dot(hi, exp_ref[...]) + _dot(lo, exp_ref[...])


def _rms_rows(x, width):
    return lax.rsqrt(jnp.sum(x * x, axis=-1, keepdims=True) * (1.0 / width) + EPS)


def _lane_tile(x, reps):
    return x if reps == 1 else jnp.concatenate([x] * reps, axis=1)


def _bucket_np(n):
    n = np.maximum(n, 0)
    max_exact = N_BUCKETS // 2
    nf = np.maximum(n, 1).astype(np.float32)
    large = max_exact + (np.log(nf / max_exact) / math.log(MAX_DISTANCE / max_exact)
                         * (N_BUCKETS - max_exact)).astype(np.int32)
    large = np.minimum(large, N_BUCKETS - 1)
    return np.where(n < max_exact, n, large).astype(np.int32)


def _prep_kernel(bk_ref, rt_ref, lq1_ref, lk1_ref, lq2_ref, lk2_ref, tile_ref, lam_ref, *, lam_init):
    bk = bk_ref[0]
    rt = rt_ref[0]
    far = rt[:, N_BUCKETS - 1:N_BUCKETS]
    acc = jnp.zeros((PAGE, LANES), F32)
    for b in range(N_BUCKETS):
        acc = jnp.where(bk == b, rt[:, b:b + 1] - far, acc)
    tile_ref[0] = jnp.where(bk < 0, NEG_INF, acc)
    s1 = jnp.sum(lq1_ref[...] * lk1_ref[...], axis=-1, keepdims=True)
    s2 = jnp.sum(lq2_ref[...] * lk2_ref[...], axis=-1, keepdims=True)
    lam = jnp.exp(s1) - jnp.exp(s2) + lam_init
    lam_ref[...] = jnp.broadcast_to(lam, lam_ref.shape)


def _bias_prep(bk, rt, lq1, lk1, lq2, lk2, lam_init):
    n = bk.shape[0]
    vec = pl.BlockSpec((1, HALF), lambda i: (0, 0))
    return pl.pallas_call(
        functools.partial(_prep_kernel, lam_init=lam_init),
        grid=(n,),
        in_specs=[pl.BlockSpec((1, PAGE, LANES), lambda i: (i, 0, 0)),
                  pl.BlockSpec((1, PAGE, N_BUCKETS), lambda i: (i, 0, 0)),
                  vec, vec, vec, vec],
        out_specs=[pl.BlockSpec((1, PAGE, LANES), lambda i: (i, 0, 0)),
                   pl.BlockSpec((8, LANES), lambda i: (0, 0))],
        out_shape=[jax.ShapeDtypeStruct((n, PAGE, LANES), F32),
                   jax.ShapeDtypeStruct((8, LANES), F32)],
        compiler_params=_cparams(("arbitrary",)),
        name="bias_prep",
    )(bk, rt, lq1, lk1, lq2, lk2)


def _proj_kernel(x_ref, ct_ref, st_ref, gattn_ref, win_ref, gqa_ref, wq_ref, gq_ref, gkv_ref, wuk_ref,
                 gkn_ref, gkr_ref, gdq_ref, gdk_ref,
                 segq_ref, expq_ref, segk_ref, expk_ref, segdq_ref, expdq_ref, segdk_ref, expdk_ref,
                 c_ref, kr_ref, kd_ref, vd_ref, qm_ref, km_ref, ca_ref, qd_ref, kdb_ref, va_ref,
                 *, d_model, q_lora, kv_lora, n_heads, n_dheads, n_kvheads):
    x = x_ref[...]
    ct = ct_ref[...]
    st = st_ref[...]
    xn = x * _rms_rows(x, d_model) * gattn_ref[...]
    p = _dot(xn.astype(BF16), win_ref[...])
    o_kv = q_lora
    o_kr = o_kv + kv_lora
    o_dq = o_kr + LANES
    o_dk = o_dq + n_dheads * LANES
    o_dv = o_dk + n_kvheads * LANES
    qa = p[:, :q_lora]
    kv = p[:, o_kv:o_kr]
    krd = p[:, o_kr:o_dq]
    dq = p[:, o_dq:o_dk]
    dk = p[:, o_dk:o_dv]
    dv = p[:, o_dv:o_dv + n_kvheads * LANES]

    def rope(v):
        return v * ct + pltpu.roll(v, 32, 1) * st

    qan = qa * _rms_rows(qa, q_lora) * gqa_ref[...]
    q = _dot(qan.astype(BF16), wq_ref[...])
    qg = q * _seg_rsqrt(q * q, segq_ref, expq_ref, QK) * gq_ref[...]
    pieces = []
    for h in range(n_heads):
        pieces.append(qg[:, h * 256:h * 256 + LANES])
        pieces.append(rope(qg[:, h * 256 + LANES:(h + 1) * 256]))
    qm_ref[...] = jnp.concatenate(pieces, axis=1).astype(BF16)

    c = kv * _rms_rows(kv, kv_lora) * gkv_ref[...]
    c_ref[...] = c
    cb = c.astype(BF16)
    ones = jnp.ones((x.shape[0], LANES), BF16)
    ca_ref[...] = jnp.concatenate([cb, ones], axis=1)
    kr = krd[:, :ROPE]
    kr_ref[...] = kr

    kn = _dot(cb, wuk_ref[...])
    kr_ssq = jnp.sum(kr * kr, axis=-1, keepdims=True)
    rk = _seg_rsqrt(kn * kn, segk_ref, expk_ref, QK, extra=kr_ssq)
    kng = kn * rk * gkn_ref[...]
    krr = rope(krd * gkr_ref[...])
    pieces = []
    for h in range(n_heads):
        pieces.append(kng[:, h * LANES:(h + 1) * LANES])
        pieces.append(krr * rk[:, h * LANES:(h + 1) * LANES])
    km_ref[...] = jnp.concatenate(pieces, axis=1).astype(BF16)

    qd = dq * _seg_rsqrt(dq * dq, segdq_ref, expdq_ref, HALF) * gdq_ref[...]
    lane = lax.broadcasted_iota(jnp.int32, (x.shape[0], LANES), 1)
    pieces = []
    for h in range(n_dheads):
        blk = qd[:, h * LANES:(h + 1) * LANES]
        pieces.append(jnp.where(lane < HALF, blk, 0.0))
        pieces.append(jnp.where(lane >= HALF, blk, 0.0))
    qd_ref[...] = jnp.concatenate(pieces, axis=1).astype(BF16)
    kd = dk * _seg_rsqrt(dk * dk, segdk_ref, expdk_ref, HALF) * gdk_ref[...]
    kd_ref[...] = kd
    kdb_ref[...] = kd.astype(BF16)
    vd_ref[...] = dv
    dvb = dv.astype(BF16)
    va_ref[...] = jnp.concatenate([t for g in range(n_kvheads)
                                   for t in (dvb[:, g * LANES:(g + 1) * LANES], ones)], axis=1)


def _seg_mats(width, seg, n_seg, skip=None):
    m = np.zeros((width, LANES), np.float32)
    for i in range(width):
        s = i // seg
        if s < n_seg and not (skip is not None and skip(i)):
            m[i, s] = 1.0
    e = np.zeros((LANES, width), np.float32)
    for i in range(width):
        s = i // seg
        if s < n_seg:
            e[s, i] = 1.0
    return jnp.asarray(m, BF16), jnp.asarray(e, BF16)


def _project(h_flat, ct, st, prm, dims):
    nf, d_model = h_flat.shape
    n_heads, n_dheads, n_kvheads = dims["n_heads"], dims["n_dheads"], dims["n_kvheads"]
    q_lora, kv_lora = dims["q_lora"], dims["kv_lora"]
    row = lambda w: pl.BlockSpec((TM, w), lambda i: (i, 0))
    full = lambda a: pl.BlockSpec(a.shape, lambda i: (0,) * a.ndim)
    consts = [prm["g_attn"], prm["w_in"], prm["g_qa"], prm["w_q"], prm["g_q"], prm["g_kv"], prm["w_uk"],
              prm["g_kn"], prm["g_kr"], prm["g_dq"], prm["g_dk"],
              prm["segq"], prm["expq"], prm["segk"], prm["expk"], prm["segdq"], prm["expdq"],
              prm["segdk"], prm["expdk"]]
    kvw = n_kvheads * LANES
    out_w = [(kv_lora, F32), (ROPE, F32), (kvw, F32), (kvw, F32),
             (n_heads * 256, BF16), (n_heads * 256, BF16), (kv_lora + LANES, BF16),
             (n_dheads * 256, BF16), (kvw, BF16), (2 * kvw, BF16)]
    return pl.pallas_call(
        functools.partial(_proj_kernel, d_model=d_model, q_lora=q_lora, kv_lora=kv_lora,
                          n_heads=n_heads, n_dheads=n_dheads, n_kvheads=n_kvheads),
        grid=(nf // TM,),
        in_specs=[row(d_model), row(LANES), row(LANES)] + [full(a) for a in consts],
        out_specs=[row(w) for w, _ in out_w],
        out_shape=[jax.ShapeDtypeStruct((nf, w), dt) for w, dt in out_w],
        compiler_params=_cparams(("arbitrary",)),
        name="project",
    )(h_flat, ct, st, *consts)


def _sample_q_kernel(qm_ref, gkn_ref, wukt_ref, perm_ref, qabs_ref, qf_ref, *, n_heads):
    for h in range(n_heads):
        qn = qm_ref[:, h * 256:h * 256 + LANES].astype(F32) * gkn_ref[...]
        qabs_ref[:, h * 256:(h + 1) * 256] = _dot(qn.astype(BF16), wukt_ref[h]).astype(BF16)
        qr = qm_ref[:, h * 256 + LANES:(h + 1) * 256]
        qf_ref[:, h * LANES:(h + 1) * LANES] = _dot(qr, perm_ref[...]).astype(BF16)


def _sample_q(qm, gkn, wukt, perm, row0, ns, n_heads):
    blk0 = row0 // LANES
    return pl.pallas_call(
        functools.partial(_sample_q_kernel, n_heads=n_heads),
        grid=(ns // LANES,),
        in_specs=[pl.BlockSpec((LANES, n_heads * 256), lambda i: (blk0 + i, 0)),
                  pl.BlockSpec(gkn.shape, lambda i: (0, 0)),
                  pl.BlockSpec(wukt.shape, lambda i: (0, 0, 0)),
                  pl.BlockSpec(perm.shape, lambda i: (0, 0))],
        out_specs=[pl.BlockSpec((LANES, n_heads * 256), lambda i: (i, 0)),
                   pl.BlockSpec((LANES, n_heads * LANES), lambda i: (i, 0))],
        out_shape=[jax.ShapeDtypeStruct((ns, n_heads * 256), BF16),
                   jax.ShapeDtypeStruct((ns, n_heads * LANES), BF16)],
        compiler_params=_cparams(("arbitrary",)),
        name="sample_q",
    )(qm, gkn, wukt, perm)


def _prompt_attn_kernel(qm_ref, qd_ref, km_ref, ca_ref, kd_ref, va_ref, wuv_ref, b0_ref, b1_ref,
                        gsub_ref, lam_ref, cat_ref, m1, a1, m2, a2, qs_sc,
                        *, n_heads, n_kvheads, per_kv, kv_lora, sub_scale):
    i = pl.program_id(1)
    row = lax.broadcasted_iota(jnp.int32, (PAGE, PAGE), 0)
    col = lax.broadcasted_iota(jnp.int32, (PAGE, PAGE), 1)
    causal = col <= row
    n_stack = per_kv * 2
    va_w = 2 * LANES

    m1[...] = jnp.full(m1.shape, NEG_INF, F32)
    a1[...] = jnp.zeros(a1.shape, F32)
    m2[...] = jnp.full(m2.shape, NEG_INF, F32)
    a2[...] = jnp.zeros(a2.shape, F32)
    for g in range(n_kvheads):
        for t in range(n_stack):
            c0 = (g * n_stack + t) * LANES
            qs_sc[g, t * PAGE:(t + 1) * PAGE, :] = qd_ref[:, c0:c0 + LANES]

    def online(s, m_ref, a_ref, idx, v, reps):
        m_old = m_ref[idx]
        m_new = jnp.maximum(m_old, jnp.max(s, axis=-1, keepdims=True))
        alpha = jnp.exp(m_old - m_new)
        p = jnp.exp(s - m_new)
        a_ref[idx] = _lane_tile(alpha, reps) * a_ref[idx] + _dot(p.astype(BF16), v)
        m_ref[idx] = m_new

    def block(j, bias_ref, masked):
        r0 = pl.multiple_of(j * PAGE, PAGE)
        ca = ca_ref[pl.ds(r0, PAGE), :]
        for h in range(n_heads):
            s = _dot_nt(qm_ref[:, h * 256:(h + 1) * 256], km_ref[pl.ds(r0, PAGE), h * 256:(h + 1) * 256])
            if masked:
                s = jnp.where(causal, s, NEG_INF)
            online(s, m1, a1, h, ca, (kv_lora + LANES) // LANES)
        for g in range(n_kvheads):
            s = _dot_nt(qs_sc[g], kd_ref[pl.ds(r0, PAGE), g * LANES:(g + 1) * LANES])
            if bias_ref is not None:
                s = s + bias_ref[g]
            online(s, m2, a2, g, va_ref[pl.ds(r0, PAGE), g * va_w:(g + 1) * va_w], va_w // LANES)

    def far_body(j, c):
        block(j, None, False)
        return c

    lax.fori_loop(0, jnp.maximum(i - 1, 0), far_body, 0)

    @pl.when(i >= 1)
    def _():
        block(i - 1, b1_ref, False)

    block(i, b0_ref, True)

    for h in range(n_heads):
        acc = a1[h]
        inv = 1.0 / acc[:, kv_lora:kv_lora + LANES]
        olat = acc[:, :kv_lora] * _lane_tile(inv, kv_lora // LANES)
        cat_ref[:, h * LANES:(h + 1) * LANES] = _dot(olat.astype(BF16), wuv_ref[h]).astype(BF16)
    lam = lam_ref[0:1, 0:1]
    for g in range(n_kvheads):
        acc = a2[g]
        o = acc[:, :LANES] * (1.0 / acc[:, LANES:])
        for r in range(per_kv):
            o1 = o[(2 * r) * PAGE:(2 * r + 1) * PAGE]
            o2 = o[(2 * r + 1) * PAGE:(2 * r + 2) * PAGE]
            od = o1 - lam * o2
            odn = od * _rms_rows(od, LANES) * gsub_ref[...] * sub_scale
            c0 = (n_heads + g * per_kv + r) * LANES
            cat_ref[:, c0:c0 + LANES] = odn.astype(BF16)


def _prompt_attn(qm, qd, km, ca, kdb, va, wuv, b0, b1, gsub, lam, batch, lp, dims, sub_scale):
    n_heads, n_dheads, n_kvheads = dims["n_heads"], dims["n_dheads"], dims["n_kvheads"]
    kv_lora = dims["kv_lora"]
    per_kv = n_dheads // n_kvheads
    nqb = lp // PAGE
    qrow = lambda w: pl.BlockSpec((PAGE, w), lambda b, i: (b * nqb + i, 0))
    kvrow = lambda w: pl.BlockSpec((lp, w), lambda b, i: (b, 0))
    full = lambda a: pl.BlockSpec(a.shape, lambda b, i: (0,) * a.ndim)
    cat_w = (n_heads + n_dheads) * LANES
    n_stack = per_kv * 2 * PAGE
    return pl.pallas_call(
        functools.partial(_prompt_attn_kernel, n_heads=n_heads, n_kvheads=n_kvheads, per_kv=per_kv,
                          kv_lora=kv_lora, sub_scale=sub_scale),
        grid=(batch, nqb),
        in_specs=[qrow(n_heads * 256), qrow(n_dheads * 256), kvrow(n_heads * 256), kvrow(ca.shape[1]),
                  kvrow(n_kvheads * LANES), kvrow(va.shape[1]),
                  full(wuv), full(b0), full(b1), full(gsub), full(lam)],
        out_specs=qrow(cat_w),
        out_shape=jax.ShapeDtypeStruct((batch * lp, cat_w), BF16),
        scratch_shapes=[pltpu.VMEM((n_heads, PAGE, LANES), F32),
                        pltpu.VMEM((n_heads, PAGE, kv_lora + LANES), F32),
                        pltpu.VMEM((n_kvheads, n_stack, LANES), F32),
                        pltpu.VMEM((n_kvheads, n_stack, 2 * LANES), F32),
                        pltpu.VMEM((n_kvheads, n_stack, LANES), BF16)],
        compiler_params=_cparams(("arbitrary", "arbitrary")),
        name="prompt_attn",
    )(qm, qd, km, ca, kdb, va, wuv, b0, b1, gsub, lam)


def _decode_kernel(pt_ref, lat_hbm, krp_hbm, dk_hbm, dv_hbm,
                   qabs_ref, lhs2_ref, qdl_ref, cnew_ref, krnew_ref, kdnew_ref, vdnew_ref,
                   wukt_ref, cst_ref, cstn_ref, ggt_ref, tlast_ref, tnew_ref, mnew_ref, lam_ref,
                   olat_ref, od_ref,
                   cbuf, rbuf, kbuf, vbuf, sem, lhs1, m1, l1, a1, m2, l2, a2,
                   *, layer, n_seq, n_chunks, pages, n_heads, n_kvheads, s_len):
    b = pl.program_id(0)
    t_chunk = pages * PAGE
    hs = n_heads * s_len
    n_nope = n_heads * NOPE
    ds_rows = m2.shape[1]

    def page_copies(seq, chunk, slot):
        cps = []
        for p in range(pages):
            pg = pt_ref[seq, chunk * pages + p]
            cps.append(pltpu.make_async_copy(lat_hbm.at[layer, pg],
                                             cbuf.at[slot, pl.ds(p * PAGE, PAGE), :], sem.at[slot]))
            cps.append(pltpu.make_async_copy(krp_hbm.at[layer, pg],
                                             rbuf.at[slot, :, pl.ds(p * PAGE, PAGE)], sem.at[slot]))
            cps.append(pltpu.make_async_copy(dk_hbm.at[layer, pg],
                                             kbuf.at[slot, pl.ds(p * n_kvheads * PAGE, n_kvheads * PAGE), :],
                                             sem.at[slot]))
            cps.append(pltpu.make_async_copy(dv_hbm.at[layer, pg],
                                             vbuf.at[slot, pl.ds(p * n_kvheads * PAGE, n_kvheads * PAGE), :],
                                             sem.at[slot]))
        return cps

    def start_chunk(seq, chunk, slot):
        for cp in page_copies(seq, chunk, slot):
            cp.start()

    def wait_chunk(seq, chunk, slot):
        for cp in page_copies(seq, chunk, slot):
            cp.wait()

    @pl.when(b == 0)
    def _():
        start_chunk(0, 0, 0)
        lhs1[0:n_nope, :] = wukt_ref[...]

    lhs1[n_nope:n_nope + hs, :] = qabs_ref[0]
    m1[...] = jnp.full(m1.shape, NEG_INF, F32)
    l1[...] = jnp.zeros(l1.shape, F32)
    a1[...] = jnp.zeros(a1.shape, F32)
    m2[...] = jnp.full(m2.shape, NEG_INF, F32)
    l2[...] = jnp.zeros(l2.shape, F32)
    a2[...] = jnp.zeros(a2.shape, F32)

    lhs2 = lhs2_ref[0]
    ggt = ggt_ref[...]

    def lane_fold(p):
        acc = p[:, 0:LANES]
        for k in range(1, p.shape[1] // LANES):
            acc = acc + p[:, k * LANES:(k + 1) * LANES]
        return acc

    def online(s, m_ref, l_ref, a_ref, idx, v):
        reps = s.shape[1] // LANES
        m_old = m_ref[idx]
        m_new = jnp.maximum(m_old, jnp.max(s, axis=-1, keepdims=True))
        alpha = jnp.exp(m_old - m_new)
        p = jnp.exp(s - _lane_tile(m_new, reps))
        l_ref[idx] = alpha * l_ref[idx] + lane_fold(p)
        a_ref[idx] = _lane_tile(alpha, a_ref.shape[-1] // LANES) * a_ref[idx] + _dot(p.astype(BF16), v)
        m_ref[idx] = m_new

    def attend(c32, krt, cst, kd_g, vd_g, mla_mask, diff_bias):
        reps = c32.shape[0] // LANES
        cbk = c32.astype(BF16)
        kall = _dot_nt(lhs1[...], cbk)
        xxt = jnp.concatenate([krt, krt], axis=0)
        f2t = jnp.concatenate([xxt * _lane_tile(ggt, reps) * cst, xxt * xxt], axis=0).astype(BF16)
        s2 = _dot(lhs2, f2t)
        krsq = s2[hs:hs + 8]
        rows = []
        for h in range(n_heads):
            kt = kall[h * NOPE:(h + 1) * NOPE]
            nsq = jnp.sum(kt * kt, axis=0, keepdims=True)
            rinv = lax.rsqrt((nsq + krsq) * (1.0 / QK) + EPS)
            r0 = n_nope + h * s_len
            rows.append((kall[r0:r0 + s_len] + s2[h * s_len:(h + 1) * s_len]) * rinv)
        s = jnp.concatenate(rows, axis=0)
        if mla_mask is not None:
            s = s + mla_mask
        online(s, m1, l1, a1, slice(None), cbk)
        for g in range(n_kvheads):
            sg = _dot_nt(qdl_ref[0, g], kd_g[g].astype(BF16))
            if diff_bias is not None:
                sg = sg + diff_bias[g]
            online(sg, m2, l2, a2, g, vd_g[g].astype(BF16))

    def chunk_step(chunk, with_bias):
        gidx = b * n_chunks + chunk
        slot = lax.rem(gidx, 2)
        wait_chunk(b, chunk, slot)
        nxt = chunk + 1

        @pl.when(nxt < n_chunks)
        def _():
            start_chunk(b, nxt, 1 - slot)

        @pl.when(jnp.logical_and(nxt == n_chunks, b + 1 < n_seq))
        def _():
            start_chunk(b + 1, 0, 1 - slot)

        kd_g = [kbuf[slot, pl.ds(g, t_chunk, stride=n_kvheads), :] for g in range(n_kvheads)]
        vd_g = [vbuf[slot, pl.ds(g, t_chunk, stride=n_kvheads), :] for g in range(n_kvheads)]
        bias = None
        if with_bias:
            zeros = jnp.zeros((ds_rows, t_chunk - PAGE), F32)
            bias = [jnp.concatenate([zeros, tlast_ref[g]], axis=1) if t_chunk > PAGE else tlast_ref[g]
                    for g in range(n_kvheads)]
        attend(cbuf[slot], rbuf[slot], cst_ref[chunk], kd_g, vd_g, None, bias)

    def body(chunk, c):
        chunk_step(chunk, False)
        return c

    lax.fori_loop(0, n_chunks - 1, body, 0)
    chunk_step(n_chunks - 1, True)

    pad = PAGE - s_len
    padr = lambda x: jnp.concatenate([x, jnp.zeros((pad, x.shape[1]), x.dtype)], axis=0)
    c_new = padr(cnew_ref[0])
    kd_new = padr(kdnew_ref[0])
    vd_new = padr(vdnew_ref[0])
    kd_g = [kd_new[:, g * LANES:(g + 1) * LANES] for g in range(n_kvheads)]
    vd_g = [vd_new[:, g * LANES:(g + 1) * LANES] for g in range(n_kvheads)]
    attend(c_new, krnew_ref[0], cstn_ref[...], kd_g, vd_g, mnew_ref[...], [tnew_ref[g] for g in range(n_kvheads)])

    olat_ref[0] = a1[...] * (1.0 / jnp.sum(l1[...], axis=-1, keepdims=True))
    lam = lam_ref[0:1, 0:1]
    half = ds_rows // 2
    for g in range(n_kvheads):
        o = a2[g] * (1.0 / jnp.sum(l2[g], axis=-1, keepdims=True))
        od_ref[0, g] = o[:half] - lam * o[half:]


def _decode_attn(page_table, lat, krp, dkp, dvp, qabs, lhs2, qdl, c_new, krt_new, kd_new, vd_new,
                 wukt, cst, cst_new, ggt, tlast, tnew, mnew, lam, layer, dims, pages):
    n_seq, n_pages = page_table.shape
    n_heads, n_kvheads = dims["n_heads"], dims["n_kvheads"]
    s_len = c_new.shape[1]
    kv_lora = c_new.shape[2]
    n_chunks = n_pages // pages
    t_chunk = pages * PAGE
    hs = n_heads * s_len
    ds_rows = qdl.shape[2]
    seq3 = lambda a: pl.BlockSpec((1,) + a.shape[1:], lambda b, pt: (b,) + (0,) * (a.ndim - 1))
    full = lambda a: pl.BlockSpec(a.shape, lambda b, pt: (0,) * a.ndim)
    hbm = pl.BlockSpec(memory_space=pl.ANY)
    grid_spec = pltpu.PrefetchScalarGridSpec(
        num_scalar_prefetch=1,
        grid=(n_seq,),
        in_specs=[hbm, hbm, hbm, hbm,
                  seq3(qabs), seq3(lhs2), seq3(qdl), seq3(c_new), seq3(krt_new), seq3(kd_new), seq3(vd_new),
                  full(wukt), full(cst), full(cst_new), full(ggt), full(tlast), full(tnew), full(mnew), full(lam)],
        out_specs=[pl.BlockSpec((1, hs, kv_lora), lambda b, pt: (b, 0, 0)),
                   pl.BlockSpec((1, n_kvheads, ds_rows // 2, LANES), lambda b, pt: (b, 0, 0, 0))],
        scratch_shapes=[pltpu.VMEM((2, t_chunk, kv_lora), F32),
                        pltpu.VMEM((2, ROPE, t_chunk), F32),
                        pltpu.VMEM((2, n_kvheads * t_chunk, LANES), F32),
                        pltpu.VMEM((2, n_kvheads * t_chunk, LANES), F32),
                        pltpu.SemaphoreType.DMA((2,)),
                        pltpu.VMEM((n_heads * NOPE + hs, kv_lora), BF16),
                        pltpu.VMEM((hs, LANES), F32), pltpu.VMEM((hs, LANES), F32), pltpu.VMEM((hs, kv_lora), F32),
                        pltpu.VMEM((n_kvheads, ds_rows, LANES), F32), pltpu.VMEM((n_kvheads, ds_rows, LANES), F32),
                        pltpu.VMEM((n_kvheads, ds_rows, LANES), F32)],
    )
    return pl.pallas_call(
        functools.partial(_decode_kernel, layer=layer, n_seq=n_seq, n_chunks=n_chunks, pages=pages,
                          n_heads=n_heads, n_kvheads=n_kvheads, s_len=s_len),
        grid_spec=grid_spec,
        out_shape=[jax.ShapeDtypeStruct((n_seq, hs, kv_lora), F32),
                   jax.ShapeDtypeStruct((n_seq, n_kvheads, ds_rows // 2, LANES), F32)],
        compiler_params=_cparams(("arbitrary",)),
        name="decode_attn",
    )(page_table, lat, krp, dkp, dvp, qabs, lhs2, qdl, c_new, krt_new, kd_new, vd_new,
      wukt, cst, cst_new, ggt, tlast, tnew, mnew, lam)


def _sample_post_kernel(olat_ref, od_ref, wuv_ref, gsub_ref, cat_ref, *, n_heads, n_dheads, kv_lora, sub_scale):
    for h in range(n_heads):
        o = olat_ref[:, h * kv_lora:(h + 1) * kv_lora].astype(BF16)
        cat_ref[:, h * LANES:(h + 1) * LANES] = _dot(o, wuv_ref[h]).astype(BF16)
    for h in range(n_dheads):
        od = od_ref[:, h * LANES:(h + 1) * LANES]
        odn = od * _rms_rows(od, LANES) * gsub_ref[...] * sub_scale
        cat_ref[:, (n_heads + h) * LANES:(n_heads + h + 1) * LANES] = odn.astype(BF16)


def _sample_post(olat, od, wuv, gsub, n_rows, dims, sub_scale):
    ns = olat.shape[0]
    n_heads, n_dheads = dims["n_heads"], dims["n_dheads"]
    kv_lora = dims["kv_lora"]
    cat_w = (n_heads + n_dheads) * LANES
    return pl.pallas_call(
        functools.partial(_sample_post_kernel, n_heads=n_heads, n_dheads=n_dheads, kv_lora=kv_lora,
                          sub_scale=sub_scale),
        grid=(ns // LANES,),
        in_specs=[pl.BlockSpec((LANES, olat.shape[1]), lambda i: (i, 0)),
                  pl.BlockSpec((LANES, od.shape[1]), lambda i: (i, 0)),
                  pl.BlockSpec(wuv.shape, lambda i: (0, 0, 0)),
                  pl.BlockSpec(gsub.shape, lambda i: (0, 0))],
        out_specs=pl.BlockSpec((LANES, cat_w), lambda i: (i, 0)),
        out_shape=jax.ShapeDtypeStruct((n_rows, cat_w), BF16),
        compiler_params=_cparams(("arbitrary",)),
        name="sample_post",
    )(olat, od, wuv, gsub)


def _merge_kernel(catp_ref, cats_ref, h_ref, wo_ref, gffn_ref, wrh_ref, wrl_ref, br_ref,
                  h1_ref, hn_ref, ti_ref, tw_ref, *, d_model, n_experts, n_pblocks):
    cat = jnp.where(pl.program_id(0) < n_pblocks, catp_ref[...], cats_ref[...])
    h1 = h_ref[...] + _dot(cat, wo_ref[...])
    h1_ref[...] = h1
    hn = h1 * _rms_rows(h1, d_model) * gffn_ref[...]
    hn_ref[...] = hn
    hi, lo = _split_bf16(hn)
    logits = _dot(hi, wrh_ref[...]) + _dot(lo, wrh_ref[...]) + _dot(hi, wrl_ref[...]) + br_ref[...]
    lane = lax.broadcasted_iota(jnp.int32, logits.shape, 1)
    lane_f = lane.astype(F32)
    x = jnp.where(lane < n_experts, logits, NEG_INF)
    vals, idxs = [], []
    for _ in range(TOP_K):
        mk = jnp.max(x, axis=-1, keepdims=True)
        ik = jnp.min(jnp.where(x == mk, lane_f, float(LANES)), axis=-1, keepdims=True).astype(jnp.int32)
        vals.append(mk)
        idxs.append(ik)
        x = jnp.where(lane == ik, NEG_INF, x)
    es = [jnp.exp(v - vals[0]) for v in vals]
    den = es[0]
    for e in es[1:]:
        den = den + e
    inv = 1.0 / den
    ti = jnp.zeros(logits.shape, jnp.int32)
    tw = jnp.zeros(logits.shape, F32)
    for k in range(TOP_K):
        ti = jnp.where(lane == k, idxs[k], ti)
        tw = jnp.where(lane == k, es[k] * inv, tw)
    ti_ref[...] = ti
    tw_ref[...] = tw


def _merge(cat_p, cat_s, h_flat, wo, gffn, wrh, wrl, br, n_experts):
    nf, d_model = h_flat.shape
    n_pblocks = cat_p.shape[0] // TM
    n_sblocks = cat_s.shape[0] // TM
    row = lambda w: pl.BlockSpec((TM, w), lambda i: (i, 0))
    full = lambda a: pl.BlockSpec(a.shape, lambda i: (0,) * a.ndim)
    return pl.pallas_call(
        functools.partial(_merge_kernel, d_model=d_model, n_experts=n_experts, n_pblocks=n_pblocks),
        grid=(nf // TM,),
        in_specs=[pl.BlockSpec((TM, cat_p.shape[1]), lambda i: (jnp.minimum(i, n_pblocks - 1), 0)),
                  pl.BlockSpec((TM, cat_s.shape[1]),
                               lambda i: (jnp.clip(i - n_pblocks, 0, n_sblocks - 1), 0)),
                  row(d_model), full(wo), full(gffn), full(wrh), full(wrl), full(br)],
        out_specs=[row(d_model), row(d_model), row(LANES), row(LANES)],
        out_shape=[jax.ShapeDtypeStruct((nf, d_model), F32), jax.ShapeDtypeStruct((nf, d_model), F32),
                   jax.ShapeDtypeStruct((nf, LANES), jnp.int32), jax.ShapeDtypeStruct((nf, LANES), F32)],
        compiler_params=_cparams(("arbitrary",)),
        name="merge",
    )(cat_p, cat_s, h_flat, wo, gffn, wrh, wrl, br)


def _expert_kernel(ie_ref, in_ref, is_ref, tok_hbm, hn_hbm, wg_ref, wu_ref, bg_ref, bu_ref, wd_ref, bd_ref,
                   out_hbm, tok_smem, pend, xs, acc, gsem, tsem, osem, *, n_fc, n_items):
    w = pl.program_id(0)
    f = pl.program_id(1)
    n = in_ref[w]
    n_sub = (n + RS - 1) // RS

    def out_copy(item, t):
        r0 = pl.multiple_of(t * RS, RS)
        return pltpu.make_async_copy(acc.at[pl.ds(r0, RS), :], out_hbm.at[item, pl.ds(r0, RS), :], osem)

    def wait_pending():
        def wbody(t, c):
            out_copy(0, t).wait()
            return c

        lax.fori_loop(0, pend[0], wbody, 0)
        pend[0] = 0

    @pl.when(jnp.logical_and(w == 0, f == 0))
    def _():
        pend[0] = 0

    @pl.when(jnp.logical_and(f == 0, n > 0))
    def _():
        start = is_ref[w]
        win0 = pl.multiple_of((start // TOK_ALIGN) * (TOK_ALIGN // LANES), TOK_ALIGN // LANES)
        off = start % TOK_ALIGN
        cp = pltpu.make_async_copy(tok_hbm.at[pl.ds(win0, TOK_WIN), :], tok_smem, tsem)
        cp.start()
        cp.wait()

        def issue(r8, c):
            for u in range(8):
                r = r8 * 8 + u
                pos = off + r
                tok = tok_smem[lax.shift_right_logical(pos, 7), lax.bitwise_and(pos, LANES - 1)]
                pltpu.make_async_copy(hn_hbm.at[pl.ds(tok, 1), :], xs.at[pl.ds(r, 1), :], gsem).start()
            return c

        lax.fori_loop(0, n_sub * (RS // 8), issue, 0)
        wait_pending()

        def prep(t, c):
            r0 = pl.multiple_of(t * RS, RS)
            acc[pl.ds(r0, RS), :] = jnp.zeros((RS, acc.shape[1]), F32)
            pltpu.make_async_copy(hn_hbm.at[pl.ds(0, RS), :], xs.at[pl.ds(r0, RS), :], gsem).wait()
            return c

        lax.fori_loop(0, n_sub, prep, 0)

    @pl.when(n > 0)
    def _():
        wg = wg_ref[0, 0].astype(BF16)
        wu = wu_ref[0, 0].astype(BF16)
        wd = wd_ref[0, 0].astype(BF16)
        bg = bg_ref[0, 0]
        bu = bu_ref[0, 0]

        def sub(t, c):
            r0 = pl.multiple_of(t * RS, RS)
            x = xs[pl.ds(r0, RS), :].astype(BF16)
            gate = jnp.minimum(_dot(x, wg) + bg, SWIGLU_LIMIT)
            up = jnp.clip(_dot(x, wu) + bu, -SWIGLU_LIMIT, SWIGLU_LIMIT)
            act = (up + 1.0) * gate * jax.nn.sigmoid(SWIGLU_ALPHA * gate)
            acc[pl.ds(r0, RS), :] += _dot(act.astype(BF16), wd)
            return c

        lax.fori_loop(0, n_sub, sub, 0)

    @pl.when(jnp.logical_and(f == n_fc - 1, n > 0))
    def _():
        def emit(t, c):
            r0 = pl.multiple_of(t * RS, RS)
            acc[pl.ds(r0, RS), :] = acc[pl.ds(r0, RS), :] + bd_ref[...]
            out_copy(w, t).start()
            return c

        lax.fori_loop(0, n_sub, emit, 0)
        pend[0] = n_sub

    @pl.when(jnp.logical_and(w == n_items - 1, f == n_fc - 1))
    def _():
        wait_pending()


def _experts(item_e, item_n, item_start, tok2d, hn, w_gate_up, b_gate_up, w_down, b_down, layer):
    n_items = item_e.shape[0]
    d_model = hn.shape[1]
    d_ff = w_down.shape[2]
    n_fc = d_ff // FC
    bgu = b_gate_up.reshape(b_gate_up.shape[0], b_gate_up.shape[1], 1, 2 * d_ff)
    bd = b_down.reshape(b_down.shape[0], b_down.shape[1], 1, d_model)
    grid_spec = pltpu.PrefetchScalarGridSpec(
        num_scalar_prefetch=3,
        grid=(n_items, n_fc),
        in_specs=[pl.BlockSpec(memory_space=pl.ANY),
                  pl.BlockSpec(memory_space=pl.ANY),
                  pl.BlockSpec((1, 1, d_model, FC), lambda w, f, ie, inn, ist: (layer, ie[w], 0, f)),
                  pl.BlockSpec((1, 1, d_model, FC), lambda w, f, ie, inn, ist: (layer, ie[w], 0, n_fc + f)),
                  pl.BlockSpec((1, 1, 1, FC), lambda w, f, ie, inn, ist: (layer, ie[w], 0, f)),
                  pl.BlockSpec((1, 1, 1, FC), lambda w, f, ie, inn, ist: (layer, ie[w], 0, n_fc + f)),
                  pl.BlockSpec((1, 1, FC, d_model), lambda w, f, ie, inn, ist: (layer, ie[w], f, 0)),
                  pl.BlockSpec((None, None, 1, d_model), lambda w, f, ie, inn, ist: (layer, ie[w], 0, 0))],
        out_specs=pl.BlockSpec(memory_space=pl.ANY),
        scratch_shapes=[pltpu.SMEM((TOK_WIN, LANES), jnp.int32),
                        pltpu.SMEM((1,), jnp.int32),
                        pltpu.VMEM((CAP, d_model), F32),
                        pltpu.VMEM((CAP, d_model), F32),
                        pltpu.SemaphoreType.DMA(()), pltpu.SemaphoreType.DMA(()), pltpu.SemaphoreType.DMA(())],
    )
    return pl.pallas_call(
        functools.partial(_expert_kernel, n_fc=n_fc, n_items=n_items),
        grid_spec=grid_spec,
        out_shape=jax.ShapeDtypeStruct((n_items, CAP, d_model), F32),
        compiler_params=_cparams(("arbitrary", "arbitrary")),
        name="experts",
    )(item_e, item_n, item_start, tok2d, hn, w_gate_up, w_gate_up, bgu, bgu, w_down, bd)


def _combine_kernel(slot_ref, h1_ref, tw_ref, eo_hbm, out_ref, slot_smem, buf, ssem, gsem):
    cp = pltpu.make_async_copy(slot_ref.at[0, 0], slot_smem, ssem)
    cp.start()
    cp.wait()

    def issue(r, c):
        for k in range(TOP_K):
            pltpu.make_async_copy(eo_hbm.at[pl.ds(slot_smem[r * TOP_K + k], 1), :],
                                  buf.at[k, pl.ds(r, 1), :], gsem).start()
        return c

    lax.fori_loop(0, TC, issue, 0, unroll=2)
    for k in range(TOP_K):
        pltpu.make_async_copy(eo_hbm.at[pl.ds(0, TC), :], buf.at[k], gsem).wait()
    tw = tw_ref[...]
    out = h1_ref[...]
    for k in range(TOP_K):
        out = out + tw[:, k:k + 1] * buf[k]
    out_ref[...] = out


def _combine(slots, h1, tw, eo):
    nf, d_model = h1.shape
    row = lambda w: pl.BlockSpec((TC, w), lambda i: (i, 0))
    return pl.pallas_call(
        _combine_kernel,
        grid=(nf // TC,),
        in_specs=[pl.BlockSpec((1, 1, TC * TOP_K), lambda i: (i, 0, 0)),
                  row(d_model), row(LANES), pl.BlockSpec(memory_space=pl.ANY)],
        out_specs=row(d_model),
        out_shape=jax.ShapeDtypeStruct((nf, d_model), F32),
        scratch_shapes=[pltpu.SMEM((TC * TOP_K,), jnp.int32),
                        pltpu.VMEM((TOP_K, TC, d_model), F32),
                        pltpu.SemaphoreType.DMA(()), pltpu.SemaphoreType.DMA(())],
        compiler_params=_cparams(("arbitrary",)),
        name="combine",
    )(slots, h1, tw, eo)


def _route(ti, valid, n_experts, n_items):
    nf = ti.shape[0]
    n_flat = nf * TOP_K
    e_flat = jnp.where(valid[:, None], ti[:, :TOP_K], n_experts).reshape(-1)
    onehot = (e_flat[:, None] == jnp.arange(n_experts, dtype=jnp.int32)[None, :]).astype(jnp.int32)
    csum = jnp.cumsum(onehot, axis=0)
    counts = csum[-1]
    rank = jnp.sum(onehot * csum, axis=1) - 1
    seg_start = jnp.cumsum(counts) - counts
    items_per = (counts + CAP - 1) // CAP
    item_cum = jnp.cumsum(items_per)
    item_first = item_cum - items_per
    total = item_cum[-1]
    w = jnp.arange(n_items, dtype=jnp.int32)
    e_of_w = jnp.minimum(jnp.searchsorted(item_cum, w, side="right"), n_experts - 1).astype(jnp.int32)
    j = w - item_first[e_of_w]
    active = w < total
    e_last = e_of_w[jnp.maximum(total - 1, 0)]
    item_e = jnp.where(active, e_of_w, e_last).astype(jnp.int32)
    item_n = jnp.where(active, jnp.clip(counts[e_of_w] - j * CAP, 0, CAP), 0).astype(jnp.int32)
    item_start = jnp.where(active, seg_start[e_of_w] + j * CAP, 0).astype(jnp.int32)
    order = jnp.argsort(e_flat, stable=True).astype(jnp.int32)
    n_tok = -(-(n_flat + CAP + TOK_ALIGN) // TOK_ALIGN) * TOK_ALIGN
    tok2d = jnp.pad(order // TOP_K, (0, n_tok - n_flat)).reshape(n_tok // LANES, LANES)
    first_of = jnp.sum(onehot * item_first[None, :], axis=1)
    slot_flat = jnp.where(e_flat < n_experts, (first_of + rank // CAP) * CAP + rank % CAP, 0).astype(jnp.int32)
    return item_e, item_n, item_start, tok2d, slot_flat.reshape(nf // TC, 1, TC * TOP_K)


def _rope_tables(pos):
    half = ROPE // 2
    inv = jnp.power(ROPE_THETA, -jnp.arange(half, dtype=F32) / half)
    ang = pos.astype(F32)[:, None] * inv[None, :]
    return jnp.cos(ang), jnp.sin(ang)


def kernel(x_prompt, x_sample, cache_mla_latent, cache_mla_krope, cache_diff_k, cache_diff_v, page_table,
           meta_tokens, rel_bias, g_attn, w_in, g_qa, w_qb, g_kv, w_kvb, g_mla_q, g_mla_k, g_diff_q, g_diff_k,
           lambda_q1, lambda_k1, lambda_q2, lambda_k2, g_subln, w_o, g_ffn, w_router, b_router,
           w_gate_up, b_gate_up, w_down, b_down):
    batch, seq, d_model = x_prompt.shape
    n_seq, s_len = x_sample.shape[0], x_sample.shape[1]
    depth = g_attn.shape[0]
    q_lora = g_qa.shape[1]
    kv_lora = g_kv.shape[1]
    n_heads = w_qb.shape[2]
    n_dheads = rel_bias.shape[1]
    n_kvheads = cache_diff_k.shape[3]
    per_kv = n_dheads // n_kvheads
    n_experts = w_router.shape[2]
    n_pool = cache_mla_latent.shape[1]
    n_pages = page_table.shape[1]
    past = n_pages * PAGE
    assert depth == 1 and w_qb.shape[3] == QK and cache_mla_krope.shape[3] == ROPE
    assert cache_diff_k.shape[4] == 2 * HALF and cache_diff_v.shape[4] == LANES and w_kvb.shape[3] == 2 * NOPE
    assert s_len == 8 and n_heads == 8 and n_experts <= LANES
    dims = dict(n_heads=n_heads, n_dheads=n_dheads, n_kvheads=n_kvheads, q_lora=q_lora, kv_lora=kv_lora)
    layer = 0
    lam_init = 0.8 - 0.6 * math.exp(-0.3 * layer)
    sub_scale = 1.0 - lam_init
    mla_scale = QK ** -0.5
    diff_scale = HALF ** -0.5

    l_tot = seq + N_META
    lp = -(-l_tot // PAGE) * PAGE
    n_p = batch * lp
    n_s = n_seq * s_len
    assert n_s % LANES == 0 and n_p % TM == 0
    nf = -(-(n_p + n_s) // TM) * TM
    meta = jnp.broadcast_to(meta_tokens.astype(F32)[None], (batch, N_META, d_model))
    h_p = jnp.concatenate([meta, x_prompt, jnp.zeros((batch, lp - l_tot, d_model), F32)], axis=1)
    h_flat = jnp.concatenate([h_p.reshape(n_p, d_model), x_sample.reshape(n_s, d_model),
                              jnp.zeros((nf - n_p - n_s, d_model), F32)], axis=0)
    pos = jnp.concatenate([jnp.tile(jnp.arange(lp, dtype=jnp.int32), batch),
                           jnp.tile(past + jnp.arange(s_len, dtype=jnp.int32), n_seq),
                           jnp.zeros((nf - n_p - n_s,), jnp.int32)])
    cos, sin = _rope_tables(pos)
    z = jnp.zeros_like(cos)
    ct = jnp.concatenate([cos, cos, z, z], axis=1)
    st = jnp.concatenate([-sin, sin, z, z], axis=1)
    row_idx = jnp.arange(nf, dtype=jnp.int32)
    valid = jnp.where(row_idx < n_p, (row_idx % lp) < l_tot, row_idx < n_p + n_s)

    w_in_l = w_in[layer]
    o = np.cumsum([0, q_lora, kv_lora, ROPE, n_dheads * 2 * HALF, n_kvheads * 2 * HALF, n_kvheads * LANES])
    w_in_p = jnp.concatenate([w_in_l[:, o[0]:o[2]], w_in_l[:, o[2]:o[3]], w_in_l[:, o[2]:o[3]],
                              w_in_l[:, o[3]:o[6]]], axis=1).astype(BF16)
    wq = w_qb[layer]
    w_q = jnp.concatenate([wq, wq[:, :, NOPE:]], axis=2).reshape(q_lora, n_heads * 256).astype(BF16)
    gq = g_mla_q[layer] * mla_scale
    g_q = jnp.tile(jnp.concatenate([gq, gq[NOPE:]]), n_heads)[None]
    gk = g_mla_k[layer]
    w_uk3 = w_kvb[layer][:, :, :NOPE]
    w_uv3 = w_kvb[layer][:, :, NOPE:]
    prm = dict(
        g_attn=g_attn[layer][None], w_in=w_in_p, g_qa=g_qa[layer][None], w_q=w_q, g_q=g_q,
        g_kv=g_kv[layer][None], w_uk=w_uk3.reshape(kv_lora, n_heads * NOPE).astype(BF16),
        g_kn=jnp.tile(gk[:NOPE], n_heads)[None], g_kr=jnp.concatenate([gk[NOPE:], gk[NOPE:]])[None],
        g_dq=jnp.tile(g_diff_q[layer] * diff_scale, 2 * n_dheads)[None],
        g_dk=jnp.tile(g_diff_k[layer], 2 * n_kvheads)[None])
    prm["segq"], prm["expq"] = _seg_mats(n_heads * 256, 256, n_heads, skip=lambda i: i % 256 >= QK)
    prm["segk"], prm["expk"] = _seg_mats(n_heads * NOPE, NOPE, n_heads)
    prm["segdq"], prm["expdq"] = _seg_mats(n_dheads * 2 * HALF, HALF, 2 * n_dheads)
    prm["segdk"], prm["expdk"] = _seg_mats(n_kvheads * 2 * HALF, HALF, 2 * n_kvheads)

    c_f, kr_f, kd_f, vd_f, qm, km, ca, qd, kdb, va = _project(h_flat, ct, st, prm, dims)

    si = np.arange(PAGE)[:, None]
    ti_ = np.arange(PAGE)[None, :]
    bk0 = np.where(si >= ti_, _bucket_np(si - ti_), -1)
    bk1 = _bucket_np(PAGE + si - ti_)
    drow = np.arange(2 * s_len * per_kv)
    d_s = (drow // per_kv) % s_len
    d_r = drow % per_kv
    n_drow = drow.shape[0]
    assert n_drow <= PAGE
    bk_last = np.zeros((PAGE, PAGE), np.int32)
    bk_last[:n_drow] = _bucket_np(PAGE + d_s[:, None] - ti_)
    bk_new = np.full((PAGE, PAGE), -1, np.int32)
    bk_new[:n_drow] = np.where((ti_ <= d_s[:, None]) & (ti_ < s_len), _bucket_np(d_s[:, None] - ti_), -1)
    bks = [bk0] * n_dheads + [bk1] * n_dheads + [bk_last] * n_kvheads + [bk_new] * n_kvheads
    rb_t = rel_bias.astype(F32).T
    head_rows = [jnp.broadcast_to(rb_t[h][None], (PAGE, N_BUCKETS)) for h in range(n_dheads)]
    dec_rows = []
    for g in range(n_kvheads):
        sel = np.zeros((PAGE,), np.int32)
        sel[:n_drow] = g * per_kv + d_r
        dec_rows.append(rb_t[jnp.asarray(sel)])
    rts = head_rows + head_rows + dec_rows + dec_rows
    tiles, lam = _bias_prep(jnp.asarray(np.stack(bks)), jnp.stack(rts), lambda_q1[layer][None],
                            lambda_k1[layer][None], lambda_q2[layer][None], lambda_k2[layer][None], lam_init)

    def stack_bias(t):
        t = t.reshape(n_kvheads, per_kv, 1, PAGE, PAGE)
        return jnp.broadcast_to(t, (n_kvheads, per_kv, 2, PAGE, PAGE)).reshape(n_kvheads, per_kv * 2 * PAGE, PAGE)

    b0 = stack_bias(tiles[:n_dheads])
    b1 = stack_bias(tiles[n_dheads:2 * n_dheads])
    tlast = tiles[2 * n_dheads:2 * n_dheads + n_kvheads, :n_drow]
    tnew = tiles[2 * n_dheads + n_kvheads:, :n_drow]

    wuv = jnp.transpose(w_uv3, (1, 0, 2)).astype(BF16)
    gsub = g_subln[layer][None]
    cat = _prompt_attn(qm, qd, km, ca, kdb, va, wuv, b0, b1, gsub, lam, batch, lp, dims, sub_scale)

    wukt3 = jnp.transpose(w_uk3, (1, 2, 0)).astype(BF16)
    perm = np.zeros((LANES, LANES), np.float32)
    for i in range(32):
        perm[i, i] = 1.0
        perm[i, 32 + i] = -1.0
        perm[32 + i, 64 + i] = 1.0
        perm[32 + i, 96 + i] = 1.0
    qabs, qf = _sample_q(qm, prm["g_kn"][:, :NOPE], wukt3, jnp.asarray(perm, BF16), n_p, n_s, n_heads)
    hs = n_heads * s_len
    to_hs = lambda a, w: a.reshape(n_seq, s_len, n_heads, w).transpose(0, 2, 1, 3).reshape(n_seq, hs, w)
    qabs_b = to_hs(qabs, kv_lora)
    qf_b = to_hs(qf, LANES)
    krsq_rows = np.zeros((8, 2 * LANES), np.float32)
    krsq_rows[:, LANES:LANES + ROPE] = 1.0
    lhs2 = jnp.concatenate([jnp.concatenate([qf_b, jnp.zeros_like(qf_b)], axis=2),
                            jnp.broadcast_to(jnp.asarray(krsq_rows, BF16)[None], (n_seq, 8, 2 * LANES))], axis=1)
    qd_s = qd[n_p:n_p + n_s].reshape(n_seq, s_len, n_kvheads, per_kv, 2, LANES)
    qdl = qd_s.transpose(0, 2, 4, 1, 3, 5).reshape(n_seq, n_kvheads, n_drow, LANES)
    seq_rows = lambda a: a[n_p:n_p + n_s].reshape(n_seq, s_len, a.shape[1])
    pages = DECODE_PAGES if n_pages % DECODE_PAGES == 0 else 1
    t_chunk = pages * PAGE
    kc, ks = _rope_tables(jnp.arange(past + PAGE, dtype=jnp.int32))
    cst_all = jnp.concatenate([kc, ks, ks, kc], axis=1).T
    cst = cst_all[:, :past].reshape(LANES, n_pages // pages, t_chunk).transpose(1, 0, 2)
    cst_new = cst_all[:, past:]
    gkr = gk[NOPE:]
    ggt = jnp.broadcast_to(jnp.concatenate([gkr, gkr])[:, None], (LANES, LANES))
    krt_new = jnp.pad(jnp.swapaxes(seq_rows(kr_f), 1, 2), ((0, 0), (0, 0), (0, PAGE - s_len)))
    j_ = np.arange(LANES)[None, :]
    s_of_row = (np.arange(hs) % s_len)[:, None]
    mnew = jnp.asarray(np.where((j_ <= s_of_row) & (j_ < s_len), 0.0, NEG_INF).astype(np.float32))
    krp = jnp.swapaxes(cache_mla_krope, 2, 3)
    dkp = cache_diff_k.reshape(depth, n_pool, PAGE * n_kvheads, 2 * HALF)
    dvp = cache_diff_v.reshape(depth, n_pool, PAGE * n_kvheads, LANES)
    olat, od = _decode_attn(page_table, cache_mla_latent, krp, dkp, dvp, qabs_b, lhs2, qdl,
                            seq_rows(c_f), krt_new, seq_rows(kd_f), seq_rows(vd_f),
                            wukt3.reshape(n_heads * NOPE, kv_lora), cst, cst_new, ggt, tlast, tnew, mnew, lam,
                            layer, dims, pages)
    olat_t = olat.reshape(n_seq, n_heads, s_len, kv_lora).transpose(0, 2, 1, 3).reshape(n_s, n_heads * kv_lora)
    od_t = od.reshape(n_seq, n_kvheads, s_len, per_kv, LANES).transpose(0, 2, 1, 3, 4).reshape(n_s, n_dheads * LANES)
    cat_s = _sample_post(olat_t, od_t, wuv, gsub, nf - n_p, dims, sub_scale)

    wr = jnp.pad(w_router[layer].astype(F32), ((0, 0), (0, LANES - n_experts)))
    wrh, wrl = _split_bf16(wr)
    br = jnp.pad(b_router[layer].astype(F32), (0, LANES - n_experts))[None]
    h1, hn, ti, tw = _merge(cat, cat_s, h_flat, w_o[layer].astype(BF16), g_ffn[layer][None], wrh, wrl, br,
                            n_experts)

    n_valid = batch * l_tot + n_s
    n_items = n_experts + -(-(n_valid * TOP_K) // CAP)
    item_e, item_n, item_start, tok2d, slots = _route(ti, valid, n_experts, n_items)
    eo = _experts(item_e, item_n, item_start, tok2d, hn, w_gate_up, b_gate_up, w_down, b_down, layer)
    out = _combine(slots, h1, tw, eo.reshape(n_items * CAP, d_model))

    y_prompt = out[:n_p].reshape(batch, lp, d_model)[:, N_META:l_tot]
    y_sample = out[n_p:n_p + n_s].reshape(n_seq, s_len, d_model)
    pr = lambda a: a[:n_p].reshape(batch, lp, a.shape[1])[:, :l_tot][None]
    sr = lambda a: a[n_p:n_p + n_s].reshape(n_seq, s_len, a.shape[1])[None]
    kv5 = lambda a: a.reshape(a.shape[:3] + (n_kvheads, a.shape[3] // n_kvheads))
    return (y_prompt, y_sample, pr(c_f), pr(kr_f), kv5(pr(kd_f)), kv5(pr(vd_f)),
            sr(c_f), sr(kr_f), kv5(sr(kd_f)), kv5(sr(vd_f)))
```

```python
import functools
import math

import numpy as np
import jax
import jax.numpy as jnp
from jax import lax
from jax.experimental import pallas as pl
from jax.experimental.pallas import tpu as pltpu

F32 = jnp.float32
BF16 = jnp.bfloat16
NEG_INF = float("-inf")

N_META = 16
ROPE_THETA = 10000.0
N_BUCKETS = 32
MAX_DISTANCE = 128
TOP_K = 4
SWIGLU_LIMIT = 7.0
SWIGLU_ALPHA = 1.702
EPS = 1e-6
PAGE = 128
NOPE = 128
ROPE = 64
QK = NOPE + ROPE
HALF = 64

LANES = 128
TM = 256
VMEM_LIMIT = 56 * 1024 * 1024
DECODE_PAGES = 16

CAP = 2048
RS = 256
FC = 256
TC = 128
TOK_ALIGN = 1024
TOK_WIN = (CAP + TOK_ALIGN) // LANES

NT_DIMS = (((1,), (1,)), ((), ()))


def _cparams(sem):
    return pltpu.CompilerParams(dimension_semantics=sem, vmem_limit_bytes=VMEM_LIMIT)


def _dot(a, b):
    return jnp.dot(a, b, preferred_element_type=F32)


def _dot_nt(a, b):
    return lax.dot_general(a, b, NT_DIMS, preferred_element_type=F32)


def _split_bf16(x):
    hi = x.astype(BF16)
    lo = (x - hi.astype(F32)).astype(BF16)
    return hi, lo


def _seg_rsqrt(sq, seg_ref, exp_ref, width, extra=None):
    ssq = _dot(sq.astype(BF16), seg_ref[...])
    if extra is not None:
        ssq = ssq + extra
    rs = lax.rsqrt(ssq * (1.0 / width) + EPS)
    hi, lo = _split_bf16(rs)
    return _dot(hi, exp_ref[...]) + _dot(lo, exp_ref[...])


def _rms_rows(x, width):
    return lax.rsqrt(jnp.sum(x * x, axis=-1, keepdims=True) * (1.0 / width) + EPS)


def _lane_tile(x, reps):
    return x if reps == 1 else jnp.concatenate([x] * reps, axis=1)


def _bucket_np(n):
    n = np.maximum(n, 0)
    max_exact = N_BUCKETS // 2
    nf = np.maximum(n, 1).astype(np.float32)
    large = max_exact + (np.log(nf / max_exact) / math.log(MAX_DISTANCE / max_exact)
                         * (N_BUCKETS - max_exact)).astype(np.int32)
    large = np.minimum(large, N_BUCKETS - 1)
    return np.where(n < max_exact, n, large).astype(np.int32)


def _prep_kernel(bk_ref, rt_ref, lq1_ref, lk1_ref, lq2_ref, lk2_ref, tile_ref, lam_ref, *, lam_init):
    bk = bk_ref[0]
    rt = rt_ref[0]
    far = rt[:, N_BUCKETS - 1:N_BUCKETS]
    acc = jnp.zeros((PAGE, LANES), F32)
    for b in range(N_BUCKETS):
        acc = jnp.where(bk == b, rt[:, b:b + 1] - far, acc)
    tile_ref[0] = jnp.where(bk < 0, NEG_INF, acc)
    s1 = jnp.sum(lq1_ref[...] * lk1_ref[...], axis=-1, keepdims=True)
    s2 = jnp.sum(lq2_ref[...] * lk2_ref[...], axis=-1, keepdims=True)
    lam = jnp.exp(s1) - jnp.exp(s2) + lam_init
    lam_ref[...] = jnp.broadcast_to(lam, lam_ref.shape)


def _bias_prep(bk, rt, lq1, lk1, lq2, lk2, lam_init):
    n = bk.shape[0]
    vec = pl.BlockSpec((1, HALF), lambda i: (0, 0))
    return pl.pallas_call(
        functools.partial(_prep_kernel, lam_init=lam_init),
        grid=(n,),
        in_specs=[pl.BlockSpec((1, PAGE, LANES), lambda i: (i, 0, 0)),
                  pl.BlockSpec((1, PAGE, N_BUCKETS), lambda i: (i, 0, 0)),
                  vec, vec, vec, vec],
        out_specs=[pl.BlockSpec((1, PAGE, LANES), lambda i: (i, 0, 0)),
                   pl.BlockSpec((8, LANES), lambda i: (0, 0))],
        out_shape=[jax.ShapeDtypeStruct((n, PAGE, LANES), F32),
                   jax.ShapeDtypeStruct((8, LANES), F32)],
        compiler_params=_cparams(("arbitrary",)),
        name="bias_prep",
    )(bk, rt, lq1, lk1, lq2, lk2)


def _proj_kernel(x_ref, ct_ref, st_ref, gattn_ref, win_ref, gqa_ref, wq_ref, gq_ref, gkv_ref, wuk_ref,
                 gkn_ref, gkr_ref, gdq_ref, gdk_ref,
                 segq_ref, expq_ref, segk_ref, expk_ref, segdq_ref, expdq_ref, segdk_ref, expdk_ref,
                 c_ref, kr_ref, kd_ref, vd_ref, qm_ref, km_ref, ca_ref, qd_ref, kdb_ref, va_ref,
                 *, d_model, q_lora, kv_lora, n_heads, n_dheads, n_kvheads):
    x = x_ref[...]
    ct = ct_ref[...]
    st = st_ref[...]
    xn = x * _rms_rows(x, d_model) * gattn_ref[...]
    p = _dot(xn.astype(BF16), win_ref[...])
    o_kv = q_lora
    o_kr = o_kv + kv_lora
    o_dq = o_kr + LANES
    o_dk = o_dq + n_dheads * LANES
    o_dv = o_dk + n_kvheads * LANES
    qa = p[:, :q_lora]
    kv = p[:, o_kv:o_kr]
    krd = p[:, o_kr:o_dq]
    dq = p[:, o_dq:o_dk]
    dk = p[:, o_dk:o_dv]
    dv = p[:, o_dv:o_dv + n_kvheads * LANES]

    def rope(v):
        return v * ct + pltpu.roll(v, 32, 1) * st

    qan = qa * _rms_rows(qa, q_lora) * gqa_ref[...]
    q = _dot(qan.astype(BF16), wq_ref[...])
    qg = q * _seg_rsqrt(q * q, segq_ref, expq_ref, QK) * gq_ref[...]
    pieces = []
    for h in range(n_heads):
        pieces.append(qg[:, h * 256:h * 256 + LANES])
        pieces.append(rope(qg[:, h * 256 + LANES:(h + 1) * 256]))
    qm_ref[...] = jnp.concatenate(pieces, axis=1).astype(BF16)

    c = kv * _rms_rows(kv, kv_lora) * gkv_ref[...]
    c_ref[...] = c
    cb = c.astype(BF16)
    ones = jnp.ones((x.shape[0], LANES), BF16)
    ca_ref[...] = jnp.concatenate([cb, ones], axis=1)
    kr = krd[:, :ROPE]
    kr_ref[...] = kr

    kn = _dot(cb, wuk_ref[...])
    kr_ssq = jnp.sum(kr * kr, axis=-1, keepdims=True)
    rk = _seg_rsqrt(kn * kn, segk_ref, expk_ref, QK, extra=kr_ssq)
    kng = kn * rk * gkn_ref[...]
    krr = rope(krd * gkr_ref[...])
    pieces = []
    for h in range(n_heads):
        pieces.append(kng[:, h * LANES:(h + 1) * LANES])
        pieces.append(krr * rk[:, h * LANES:(h + 1) * LANES])
    km_ref[...] = jnp.concatenate(pieces, axis=1).astype(BF16)

    qd = dq * _seg_rsqrt(dq * dq, segdq_ref, expdq_ref, HALF) * gdq_ref[...]
    lane = lax.broadcasted_iota(jnp.int32, (x.shape[0], LANES), 1)
    pieces = []
    for h in range(n_dheads):
        blk = qd[:, h * LANES:(h + 1) * LANES]
        pieces.append(jnp.where(lane < HALF, blk, 0.0))
        pieces.append(jnp.where(lane >= HALF, blk, 0.0))
    qd_ref[...] = jnp.concatenate(pieces, axis=1).astype(BF16)
    kd = dk * _seg_rsqrt(dk * dk, segdk_ref, expdk_ref, HALF) * gdk_ref[...]
    kd_ref[...] = kd
    kdb_ref[...] = kd.astype(BF16)
    vd_ref[...] = dv
    dvb = dv.astype(BF16)
    va_ref[...] = jnp.concatenate([t for g in range(n_kvheads)
                                   for t in (dvb[:, g * LANES:(g + 1) * LANES], ones)], axis=1)


def _seg_mats(width, seg, n_seg, skip=None):
    m = np.zeros((width, LANES), np.float32)
    for i in range(width):
        s = i // seg
        if s < n_seg and not (skip is not None and skip(i)):
            m[i, s] = 1.0
    e = np.zeros((LANES, width), np.float32)
    for i in range(width):
        s = i // seg
        if s < n_seg:
            e[s, i] = 1.0
    return jnp.asarray(m, BF16), jnp.asarray(e, BF16)


def _project(h_flat, ct, st, prm, dims):
    nf, d_model = h_flat.shape
    n_heads, n_dheads, n_kvheads = dims["n_heads"], dims["n_dheads"], dims["n_kvheads"]
    q_lora, kv_lora = dims["q_lora"], dims["kv_lora"]
    row = lambda w: pl.BlockSpec((TM, w), lambda i: (i, 0))
    full = lambda a: pl.BlockSpec(a.shape, lambda i: (0,) * a.ndim)
    consts = [prm["g_attn"], prm["w_in"], prm["g_qa"], prm["w_q"], prm["g_q"], prm["g_kv"], prm["w_uk"],
              prm["g_kn"], prm["g_kr"], prm["g_dq"], prm["g_dk"],
              prm["segq"], prm["expq"], prm["segk"], prm["expk"], prm["segdq"], prm["expdq"],
              prm["segdk"], prm["expdk"]]
    kvw = n_kvheads * LANES
    out_w = [(kv_lora, F32), (ROPE, F32), (kvw, F32), (kvw, F32),
             (n_heads * 256, BF16), (n_heads * 256, BF16), (kv_lora + LANES, BF16),
             (n_dheads * 256, BF16), (kvw, BF16), (2 * kvw, BF16)]
    return pl.pallas_call(
        functools.partial(_proj_kernel, d_model=d_model, q_lora=q_lora, kv_lora=kv_lora,
                          n_heads=n_heads, n_dheads=n_dheads, n_kvheads=n_kvheads),
        grid=(nf // TM,),
        in_specs=[row(d_model), row(LANES), row(LANES)] + [full(a) for a in consts],
        out_specs=[row(w) for w, _ in out_w],
        out_shape=[jax.ShapeDtypeStruct((nf, w), dt) for w, dt in out_w],
        compiler_params=_cparams(("arbitrary",)),
        name="project",
    )(h_flat, ct, st, *consts)


def _sample_q_kernel(qm_ref, gkn_ref, wukt_ref, perm_ref, qabs_ref, qf_ref, *, n_heads):
    for h in range(n_heads):
        qn = qm_ref[:, h * 256:h * 256 + LANES].astype(F32) * gkn_ref[...]
        qabs_ref[:, h * 256:(h + 1) * 256] = _dot(qn.astype(BF16), wukt_ref[h]).astype(BF16)
        qr = qm_ref[:, h * 256 + LANES:(h + 1) * 256]
        qf_ref[:, h * LANES:(h + 1) * LANES] = _dot(qr, perm_ref[...]).astype(BF16)


def _sample_q(qm, gkn, wukt, perm, row0, ns, n_heads):
    blk0 = row0 // LANES
    return pl.pallas_call(
        functools.partial(_sample_q_kernel, n_heads=n_heads),
        grid=(ns // LANES,),
        in_specs=[pl.BlockSpec((LANES, n_heads * 256), lambda i: (blk0 + i, 0)),
                  pl.BlockSpec(gkn.shape, lambda i: (0, 0)),
                  pl.BlockSpec(wukt.shape, lambda i: (0, 0, 0)),
                  pl.BlockSpec(perm.shape, lambda i: (0, 0))],
        out_specs=[pl.BlockSpec((LANES, n_heads * 256), lambda i: (i, 0)),
                   pl.BlockSpec((LANES, n_heads * LANES), lambda i: (i, 0))],
        out_shape=[jax.ShapeDtypeStruct((ns, n_heads * 256), BF16),
                   jax.ShapeDtypeStruct((ns, n_heads * LANES), BF16)],
        compiler_params=_cparams(("arbitrary",)),
        name="sample_q",
    )(qm, gkn, wukt, perm)


def _prompt_attn_kernel(qm_ref, qd_ref, km_ref, ca_ref, kd_ref, va_ref, wuv_ref, b0_ref, b1_ref,
                        gsub_ref, lam_ref, cat_ref, m1, a1, m2, a2, qs_sc,
                        *, n_heads, n_kvheads, per_kv, kv_lora, sub_scale):
    i = pl.program_id(1)
    row = lax.broadcasted_iota(jnp.int32, (PAGE, PAGE), 0)
    col = lax.broadcasted_iota(jnp.int32, (PAGE, PAGE), 1)
    causal = col <= row
    n_stack = per_kv * 2
    va_w = 2 * LANES

    m1[...] = jnp.full(m1.shape, NEG_INF, F32)
    a1[...] = jnp.zeros(a1.shape, F32)
    m2[...] = jnp.full(m2.shape, NEG_INF, F32)
    a2[...] = jnp.zeros(a2.shape, F32)
    for g in range(n_kvheads):
        for t in range(n_stack):
            c0 = (g * n_stack + t) * LANES
            qs_sc[g, t * PAGE:(t + 1) * PAGE, :] = qd_ref[:, c0:c0 + LANES]

    def soft(s, m_ref, idx):
        m_old = m_ref[idx]
        m_new = jnp.maximum(m_old, jnp.max(s, axis=-1, keepdims=True))
        m_ref[idx] = m_new
        return jnp.exp(m_old - m_new), jnp.exp(s - m_new).astype(BF16)

    def block(j, bias_ref, masked):
        r0 = pl.multiple_of(j * PAGE, PAGE)
        scores = []
        for h in range(n_heads):
            s = _dot_nt(qm_ref[:, h * 256:(h + 1) * 256], km_ref[pl.ds(r0, PAGE), h * 256:(h + 1) * 256])
            scores.append(jnp.where(causal, s, NEG_INF) if masked else s)
        for g in range(n_kvheads):
            s = _dot_nt(qs_sc[g], kd_ref[pl.ds(r0, PAGE), g * LANES:(g + 1) * LANES])
            scores.append(s if bias_ref is None else s + bias_ref[g])
        softs = [soft(scores[h], m1, h) for h in range(n_heads)]
        softs += [soft(scores[n_heads + g], m2, g) for g in range(n_kvheads)]
        ca = ca_ref[pl.ds(r0, PAGE), :]
        for h in range(n_heads):
            alpha, p = softs[h]
            a1[h] = _lane_tile(alpha, (kv_lora + LANES) // LANES) * a1[h] + _dot(p, ca)
        for g in range(n_kvheads):
            alpha, p = softs[n_heads + g]
            va = va_ref[pl.ds(r0, PAGE), g * va_w:(g + 1) * va_w]
            a2[g] = _lane_tile(alpha, va_w // LANES) * a2[g] + _dot(p, va)

    def far_body(j, c):
        block(j, None, False)
        return c

    lax.fori_loop(0, jnp.maximum(i - 1, 0), far_body, 0)

    @pl.when(i >= 1)
    def _():
        block(i - 1, b1_ref, False)

    block(i, b0_ref, True)

    for h in range(n_heads):
        acc = a1[h]
        inv = 1.0 / acc[:, kv_lora:kv_lora + LANES]
        olat = acc[:, :kv_lora] * _lane_tile(inv, kv_lora // LANES)
        cat_ref[:, h * LANES:(h + 1) * LANES] = _dot(olat.astype(BF16), wuv_ref[h]).astype(BF16)
    lam = lam_ref[0:1, 0:1]
    for g in range(n_kvheads):
        acc = a2[g]
        o = acc[:, :LANES] * (1.0 / acc[:, LANES:])
        for r in range(per_kv):
            o1 = o[(2 * r) * PAGE:(2 * r + 1) * PAGE]
            o2 = o[(2 * r + 1) * PAGE:(2 * r + 2) * PAGE]
            od = o1 - lam * o2
            odn = od * _rms_rows(od, LANES) * gsub_ref[...] * sub_scale
            c0 = (n_heads + g * per_kv + r) * LANES
            cat_ref[:, c0:c0 + LANES] = odn.astype(BF16)


def _prompt_attn(qm, qd, km, ca, kdb, va, wuv, b0, b1, gsub, lam, batch, lp, dims, sub_scale):
    n_heads, n_dheads, n_kvheads = dims["n_heads"], dims["n_dheads"], dims["n_kvheads"]
    kv_lora = dims["kv_lora"]
    per_kv = n_dheads // n_kvheads
    nqb = lp // PAGE
    qrow = lambda w: pl.BlockSpec((PAGE, w), lambda b, i: (b * nqb + i, 0))
    kvrow = lambda w: pl.BlockSpec((lp, w), lambda b, i: (b, 0))
    full = lambda a: pl.BlockSpec(a.shape, lambda b, i: (0,) * a.ndim)
    cat_w = (n_heads + n_dheads) * LANES
    n_stack = per_kv * 2 * PAGE
    return pl.pallas_call(
        functools.partial(_prompt_attn_kernel, n_heads=n_heads, n_kvheads=n_kvheads, per_kv=per_kv,
                          kv_lora=kv_lora, sub_scale=sub_scale),
        grid=(batch, nqb),
        in_specs=[qrow(n_heads * 256), qrow(n_dheads * 256), kvrow(n_heads * 256), kvrow(ca.shape[1]),
                  kvrow(n_kvheads * LANES), kvrow(va.shape[1]),
                  full(wuv), full(b0), full(b1), full(gsub), full(lam)],
        out_specs=qrow(cat_w),
        out_shape=jax.ShapeDtypeStruct((batch * lp, cat_w), BF16),
        scratch_shapes=[pltpu.VMEM((n_heads, PAGE, LANES), F32),
                        pltpu.VMEM((n_heads, PAGE, kv_lora + LANES), F32),
                        pltpu.VMEM((n_kvheads, n_stack, LANES), F32),
                        pltpu.VMEM((n_kvheads, n_stack, 2 * LANES), F32),
                        pltpu.VMEM((n_kvheads, n_stack, LANES), BF16)],
        compiler_params=_cparams(("arbitrary", "arbitrary")),
        name="prompt_attn",
    )(qm, qd, km, ca, kdb, va, wuv, b0, b1, gsub, lam)


def _decode_kernel(pt_ref, lat_hbm, krp_hbm, dk_hbm, dv_hbm,
                   qabs_ref, lhs2_ref, qdl_ref, cnew_ref, krnew_ref, kdnew_ref, vdnew_ref,
                   wukt_ref, cst_ref, cstn_ref, ggt_ref, tlast_ref, tnew_ref, mnew_ref, lam_ref,
                   olat_ref, od_ref,
                   cbuf, rbuf, kbuf, vbuf, sem, lhs1,
                   *, layer, n_seq, n_chunks, pages, n_heads, n_kvheads, s_len, kv_lora, ds_rows):
    b = pl.program_id(0)
    t_chunk = pages * PAGE
    hs = n_heads * s_len
    n_nope = n_heads * NOPE

    def page_copies(seq, chunk, slot):
        cps = []
        for p in range(pages):
            pg = pt_ref[seq, chunk * pages + p]
            cps.append(pltpu.make_async_copy(lat_hbm.at[layer, pg],
                                             cbuf.at[slot, pl.ds(p * PAGE, PAGE), :], sem.at[slot]))
            cps.append(pltpu.make_async_copy(krp_hbm.at[layer, pg],
                                             rbuf.at[slot, :, pl.ds(p * PAGE, PAGE)], sem.at[slot]))
            cps.append(pltpu.make_async_copy(dk_hbm.at[layer, pg],
                                             kbuf.at[slot, pl.ds(p * n_kvheads * PAGE, n_kvheads * PAGE), :],
                                             sem.at[slot]))
            cps.append(pltpu.make_async_copy(dv_hbm.at[layer, pg],
                                             vbuf.at[slot, pl.ds(p * n_kvheads * PAGE, n_kvheads * PAGE), :],
                                             sem.at[slot]))
        return cps

    def start_chunk(seq, chunk, slot):
        for cp in page_copies(seq, chunk, slot):
            cp.start()

    def wait_chunk(seq, chunk, slot):
        for cp in page_copies(seq, chunk, slot):
            cp.wait()

    @pl.when(b == 0)
    def _():
        start_chunk(0, 0, 0)
        lhs1[0:n_nope, :] = wukt_ref[...]

    lhs1[n_nope:n_nope + hs, :] = qabs_ref[0]

    def fresh(rows, width):
        return (jnp.full((rows, LANES), NEG_INF, F32), jnp.zeros((rows, LANES), F32),
                jnp.zeros((rows, width), F32))

    state0 = (fresh(hs, kv_lora),) + tuple(fresh(ds_rows, LANES) for _ in range(n_kvheads))

    lhs2 = lhs2_ref[0]
    ggt = ggt_ref[...]

    def lane_fold(p):
        acc = p[:, 0:LANES]
        for k in range(1, p.shape[1] // LANES):
            acc = acc + p[:, k * LANES:(k + 1) * LANES]
        return acc

    def key_products(c32, krt, cst):
        reps = c32.shape[0] // LANES
        kall = _dot_nt(lhs1[...], c32.astype(BF16))
        xxt = jnp.concatenate([krt, krt], axis=0)
        f2t = jnp.concatenate([xxt * _lane_tile(ggt, reps) * cst, xxt * xxt], axis=0).astype(BF16)
        return kall, _dot(lhs2, f2t)

    def key_norms(kall, s2):
        krsq = s2[hs:hs + 8]
        rows = []
        for h in range(n_heads):
            kt = kall[h * NOPE:(h + 1) * NOPE]
            nsq = jnp.sum(kt * kt, axis=0, keepdims=True)
            rinv = lax.rsqrt((nsq + krsq) * (1.0 / QK) + EPS)
            r0 = n_nope + h * s_len
            rows.append((kall[r0:r0 + s_len] + s2[h * s_len:(h + 1) * s_len]) * rinv)
        return jnp.concatenate(rows, axis=0)

    def diff_scores(kd_g, diff_bias):
        out = []
        for g in range(n_kvheads):
            sg = _dot_nt(qdl_ref[0, g], kd_g[g].astype(BF16))
            out.append(sg if diff_bias is None else sg + diff_bias[g])
        return out

    def soft(scores, state):
        out = []
        for s, (m_old, l_old, _) in zip(scores, state):
            m_new = jnp.maximum(m_old, jnp.max(s, axis=-1, keepdims=True))
            alpha = jnp.exp(m_old - m_new)
            p = jnp.exp(s - _lane_tile(m_new, s.shape[1] // LANES))
            out.append((m_new, alpha * l_old + lane_fold(p), alpha, p.astype(BF16)))
        return out

    def value_products(softs, state, values):
        new = []
        for (m_new, l_new, alpha, p), (_, _, a_old), v in zip(softs, state, values):
            new.append((m_new, l_new, _lane_tile(alpha, a_old.shape[1] // LANES) * a_old + _dot(p, v)))
        return tuple(new)

    def slot_of(chunk):
        return lax.rem(b * n_chunks + chunk, 3)

    def fetch(chunk):
        wait_chunk(b, chunk, slot_of(chunk))
        last = chunk + 1 == n_chunks
        nseq = jnp.where(last, jnp.minimum(b + 1, n_seq - 1), b)
        start_chunk(nseq, jnp.where(last, 0, chunk + 1), slot_of(chunk + 1))

    def chunk_keys(chunk):
        slot = slot_of(chunk)
        return key_products(cbuf[slot], rbuf[slot], cst_ref[chunk])

    def strided(buf, slot):
        return [buf[slot, pl.ds(g, t_chunk, stride=n_kvheads), :] for g in range(n_kvheads)]

    def chunk_values(chunk):
        slot = slot_of(chunk)
        return [cbuf[slot].astype(BF16)] + [v.astype(BF16) for v in strided(vbuf, slot)]

    def body(chunk, carry):
        s_prev, state = carry
        fetch(chunk)
        sd = diff_scores(strided(kbuf, slot_of(chunk - 1)), None)
        kall, s2 = chunk_keys(chunk)
        softs = soft([s_prev] + sd, state)
        state = value_products(softs, state, chunk_values(chunk - 1))
        return key_norms(kall, s2), state

    fetch(0)
    s_last, state = lax.fori_loop(1, n_chunks, body, (key_norms(*chunk_keys(0)), state0))

    zeros = jnp.zeros((ds_rows, t_chunk - PAGE), F32)
    bias = [jnp.concatenate([zeros, tlast_ref[g]], axis=1) if t_chunk > PAGE else tlast_ref[g]
            for g in range(n_kvheads)]
    sd = diff_scores(strided(kbuf, slot_of(n_chunks - 1)), bias)
    pad = PAGE - s_len
    padr = lambda x: jnp.concatenate([x, jnp.zeros((pad, x.shape[1]), x.dtype)], axis=0)
    c_new = padr(cnew_ref[0])
    kd_new = padr(kdnew_ref[0])
    vd_new = padr(vdnew_ref[0])
    kall, s2 = key_products(c_new, krnew_ref[0], cstn_ref[...])
    state = value_products(soft([s_last] + sd, state), state, chunk_values(n_chunks - 1))
    sd = diff_scores([kd_new[:, g * LANES:(g + 1) * LANES] for g in range(n_kvheads)],
                     [tnew_ref[g] for g in range(n_kvheads)])
    s_new = key_norms(kall, s2) + mnew_ref[...]
    values = [c_new.astype(BF16)] + [vd_new[:, g * LANES:(g + 1) * LANES].astype(BF16) for g in range(n_kvheads)]
    state = value_products(soft([s_new] + sd, state), state, values)

    @pl.when(b == n_seq - 1)
    def _():
        wait_chunk(b, 0, slot_of(n_chunks))

    _, l1, a1 = state[0]
    olat_ref[0] = a1 * (1.0 / jnp.sum(l1, axis=-1, keepdims=True))
    lam = lam_ref[0:1, 0:1]
    half = ds_rows // 2
    for g in range(n_kvheads):
        _, l2, a2 = state[1 + g]
        o = a2 * (1.0 / jnp.sum(l2, axis=-1, keepdims=True))
        od_ref[0, g] = o[:half] - lam * o[half:]


def _decode_attn(page_table, lat, krp, dkp, dvp, qabs, lhs2, qdl, c_new, krt_new, kd_new, vd_new,
                 wukt, cst, cst_new, ggt, tlast, tnew, mnew, lam, layer, dims, pages):
    n_seq, n_pages = page_table.shape
    n_heads, n_kvheads = dims["n_heads"], dims["n_kvheads"]
    s_len = c_new.shape[1]
    kv_lora = c_new.shape[2]
    n_chunks = n_pages // pages
    t_chunk = pages * PAGE
    hs = n_heads * s_len
    ds_rows = qdl.shape[2]
    seq3 = lambda a: pl.BlockSpec((1,) + a.shape[1:], lambda b, pt: (b,) + (0,) * (a.ndim - 1))
    full = lambda a: pl.BlockSpec(a.shape, lambda b, pt: (0,) * a.ndim)
    hbm = pl.BlockSpec(memory_space=pl.ANY)
    grid_spec = pltpu.PrefetchScalarGridSpec(
        num_scalar_prefetch=1,
        grid=(n_seq,),
        in_specs=[hbm, hbm, hbm, hbm,
                  seq3(qabs), seq3(lhs2), seq3(qdl), seq3(c_new), seq3(krt_new), seq3(kd_new), seq3(vd_new),
                  full(wukt), full(cst), full(cst_new), full(ggt), full(tlast), full(tnew), full(mnew), full(lam)],
        out_specs=[pl.BlockSpec((1, hs, kv_lora), lambda b, pt: (b, 0, 0)),
                   pl.BlockSpec((1, n_kvheads, ds_rows // 2, LANES), lambda b, pt: (b, 0, 0, 0))],
        scratch_shapes=[pltpu.VMEM((3, t_chunk, kv_lora), F32),
                        pltpu.VMEM((3, ROPE, t_chunk), F32),
                        pltpu.VMEM((3, n_kvheads * t_chunk, LANES), F32),
                        pltpu.VMEM((3, n_kvheads * t_chunk, LANES), F32),
                        pltpu.SemaphoreType.DMA((3,)),
                        pltpu.VMEM((n_heads * NOPE + hs, kv_lora), BF16)],
    )
    return pl.pallas_call(
        functools.partial(_decode_kernel, layer=layer, n_seq=n_seq, n_chunks=n_chunks, pages=pages,
                          n_heads=n_heads, n_kvheads=n_kvheads, s_len=s_len, kv_lora=kv_lora,
                          ds_rows=ds_rows),
        grid_spec=grid_spec,
        out_shape=[jax.ShapeDtypeStruct((n_seq, hs, kv_lora), F32),
                   jax.ShapeDtypeStruct((n_seq, n_kvheads, ds_rows // 2, LANES), F32)],
        compiler_params=_cparams(("arbitrary",)),
        name="decode_attn",
    )(page_table, lat, krp, dkp, dvp, qabs, lhs2, qdl, c_new, krt_new, kd_new, vd_new,
      wukt, cst, cst_new, ggt, tlast, tnew, mnew, lam)


def _sample_post_kernel(olat_ref, od_ref, wuv_ref, gsub_ref, cat_ref, *, n_heads, n_dheads, kv_lora, sub_scale):
    for h in range(n_heads):
        o = olat_ref[:, h * kv_lora:(h + 1) * kv_lora].astype(BF16)
        cat_ref[:, h * LANES:(h + 1) * LANES] = _dot(o, wuv_ref[h]).astype(BF16)
    for h in range(n_dheads):
        od = od_ref[:, h * LANES:(h + 1) * LANES]
        odn = od * _rms_rows(od, LANES) * gsub_ref[...] * sub_scale
        cat_ref[:, (n_heads + h) * LANES:(n_heads + h + 1) * LANES] = odn.astype(BF16)


def _sample_post(olat, od, wuv, gsub, n_rows, dims, sub_scale):
    ns = olat.shape[0]
    n_heads, n_dheads = dims["n_heads"], dims["n_dheads"]
    kv_lora = dims["kv_lora"]
    cat_w = (n_heads + n_dheads) * LANES
    return pl.pallas_call(
        functools.partial(_sample_post_kernel, n_heads=n_heads, n_dheads=n_dheads, kv_lora=kv_lora,
                          sub_scale=sub_scale),
        grid=(ns // LANES,),
        in_specs=[pl.BlockSpec((LANES, olat.shape[1]), lambda i: (i, 0)),
                  pl.BlockSpec((LANES, od.shape[1]), lambda i: (i, 0)),
                  pl.BlockSpec(wuv.shape, lambda i: (0, 0, 0)),
                  pl.BlockSpec(gsub.shape, lambda i: (0, 0))],
        out_specs=pl.BlockSpec((LANES, cat_w), lambda i: (i, 0)),
        out_shape=jax.ShapeDtypeStruct((n_rows, cat_w), BF16),
        compiler_params=_cparams(("arbitrary",)),
        name="sample_post",
    )(olat, od, wuv, gsub)


def _merge_kernel(catp_ref, cats_ref, h_ref, wo_ref, gffn_ref, wrh_ref, wrl_ref, br_ref,
                  h1_ref, hn_ref, ti_ref, tw_ref, *, d_model, n_experts, n_pblocks):
    cat = jnp.where(pl.program_id(0) < n_pblocks, catp_ref[...], cats_ref[...])
    h1 = h_ref[...] + _dot(cat, wo_ref[...])
    h1_ref[...] = h1
    hn = h1 * _rms_rows(h1, d_model) * gffn_ref[...]
    hn_ref[...] = hn
    hi, lo = _split_bf16(hn)
    logits = _dot(hi, wrh_ref[...]) + _dot(lo, wrh_ref[...]) + _dot(hi, wrl_ref[...]) + br_ref[...]
    lane = lax.broadcasted_iota(jnp.int32, logits.shape, 1)
    lane_f = lane.astype(F32)
    x = jnp.where(lane < n_experts, logits, NEG_INF)
    vals, idxs = [], []
    for _ in range(TOP_K):
        mk = jnp.max(x, axis=-1, keepdims=True)
        ik = jnp.min(jnp.where(x == mk, lane_f, float(LANES)), axis=-1, keepdims=True).astype(jnp.int32)
        vals.append(mk)
        idxs.append(ik)
        x = jnp.where(lane == ik, NEG_INF, x)
    es = [jnp.exp(v - vals[0]) for v in vals]
    den = es[0]
    for e in es[1:]:
        den = den + e
    inv = 1.0 / den
    ti = jnp.zeros(logits.shape, jnp.int32)
    tw = jnp.zeros(logits.shape, F32)
    for k in range(TOP_K):
        ti = jnp.where(lane == k, idxs[k], ti)
        tw = jnp.where(lane == k, es[k] * inv, tw)
    ti_ref[...] = ti
    tw_ref[...] = tw


def _merge(cat_p, cat_s, h_flat, wo, gffn, wrh, wrl, br, n_experts):
    nf, d_model = h_flat.shape
    n_pblocks = cat_p.shape[0] // TM
    n_sblocks = cat_s.shape[0] // TM
    row = lambda w: pl.BlockSpec((TM, w), lambda i: (i, 0))
    full = lambda a: pl.BlockSpec(a.shape, lambda i: (0,) * a.ndim)
    return pl.pallas_call(
        functools.partial(_merge_kernel, d_model=d_model, n_experts=n_experts, n_pblocks=n_pblocks),
        grid=(nf // TM,),
        in_specs=[pl.BlockSpec((TM, cat_p.shape[1]), lambda i: (jnp.minimum(i, n_pblocks - 1), 0)),
                  pl.BlockSpec((TM, cat_s.shape[1]),
                               lambda i: (jnp.clip(i - n_pblocks, 0, n_sblocks - 1), 0)),
                  row(d_model), full(wo), full(gffn), full(wrh), full(wrl), full(br)],
        out_specs=[row(d_model), row(d_model), row(LANES), row(LANES)],
        out_shape=[jax.ShapeDtypeStruct((nf, d_model), F32), jax.ShapeDtypeStruct((nf, d_model), F32),
                   jax.ShapeDtypeStruct((nf, LANES), jnp.int32), jax.ShapeDtypeStruct((nf, LANES), F32)],
        compiler_params=_cparams(("arbitrary",)),
        name="merge",
    )(cat_p, cat_s, h_flat, wo, gffn, wrh, wrl, br)


def _expert_kernel(ie_ref, in_ref, is_ref, tok_hbm, hn_hbm, wg_ref, wu_ref, bg_ref, bu_ref, wd_ref, bd_ref,
                   out_hbm, tok_smem, pend, xs, acc, gsem, tsem, osem, *, n_fc, n_items):
    w = pl.program_id(0)
    f = pl.program_id(1)
    n = in_ref[w]
    n_sub = (n + RS - 1) // RS

    def out_copy(item, t):
        r0 = pl.multiple_of(t * RS, RS)
        return pltpu.make_async_copy(acc.at[pl.ds(r0, RS), :], out_hbm.at[item, pl.ds(r0, RS), :], osem)

    def wait_pending():
        def wbody(t, c):
            out_copy(0, t).wait()
            return c

        lax.fori_loop(0, pend[0], wbody, 0)
        pend[0] = 0

    @pl.when(jnp.logical_and(w == 0, f == 0))
    def _():
        pend[0] = 0

    @pl.when(jnp.logical_and(f == 0, n > 0))
    def _():
        start = is_ref[w]
        win0 = pl.multiple_of((start // TOK_ALIGN) * (TOK_ALIGN // LANES), TOK_ALIGN // LANES)
        off = start % TOK_ALIGN
        cp = pltpu.make_async_copy(tok_hbm.at[pl.ds(win0, TOK_WIN), :], tok_smem, tsem)
        cp.start()
        cp.wait()

        def issue(r8, c):
            for u in range(8):
                r = r8 * 8 + u
                pos = off + r
                tok = tok_smem[lax.shift_right_logical(pos, 7), lax.bitwise_and(pos, LANES - 1)]
                pltpu.make_async_copy(hn_hbm.at[pl.ds(tok, 1), :], xs.at[pl.ds(r, 1), :], gsem).start()
            return c

        lax.fori_loop(0, n_sub * (RS // 8), issue, 0)
        wait_pending()

        def prep(t, c):
            r0 = pl.multiple_of(t * RS, RS)
            acc[pl.ds(r0, RS), :] = jnp.zeros((RS, acc.shape[1]), F32)
            pltpu.make_async_copy(hn_hbm.at[pl.ds(0, RS), :], xs.at[pl.ds(r0, RS), :], gsem).wait()
            return c

        lax.fori_loop(0, n_sub, prep, 0)

    @pl.when(n > 0)
    def _():
        wg = wg_ref[0, 0].astype(BF16)
        wu = wu_ref[0, 0].astype(BF16)
        wd = wd_ref[0, 0].astype(BF16)
        bg = bg_ref[0, 0]
        bu = bu_ref[0, 0]

        def sub(t, c):
            r0 = pl.multiple_of(t * RS, RS)
            x = xs[pl.ds(r0, RS), :].astype(BF16)
            gate = jnp.minimum(_dot(x, wg) + bg, SWIGLU_LIMIT)
            up = jnp.clip(_dot(x, wu) + bu, -SWIGLU_LIMIT, SWIGLU_LIMIT)
            act = (up + 1.0) * gate * jax.nn.sigmoid(SWIGLU_ALPHA * gate)
            acc[pl.ds(r0, RS), :] += _dot(act.astype(BF16), wd)
            return c

        lax.fori_loop(0, n_sub, sub, 0)

    @pl.when(jnp.logical_and(f == n_fc - 1, n > 0))
    def _():
        def emit(t, c):
            r0 = pl.multiple_of(t * RS, RS)
            acc[pl.ds(r0, RS), :] = acc[pl.ds(r0, RS), :] + bd_ref[...]
            out_copy(w, t).start()
            return c

        lax.fori_loop(0, n_sub, emit, 0)
        pend[0] = n_sub

    @pl.when(jnp.logical_and(w == n_items - 1, f == n_fc - 1))
    def _():
        wait_pending()


def _experts(item_e, item_n, item_start, tok2d, hn, w_gate_up, b_gate_up, w_down, b_down, layer):
    n_items = item_e.shape[0]
    d_model = hn.shape[1]
    d_ff = w_down.shape[2]
    n_fc = d_ff // FC
    bgu = b_gate_up.reshape(b_gate_up.shape[0], b_gate_up.shape[1], 1, 2 * d_ff)
    bd = b_down.reshape(b_down.shape[0], b_down.shape[1], 1, d_model)
    grid_spec = pltpu.PrefetchScalarGridSpec(
        num_scalar_prefetch=3,
        grid=(n_items, n_fc),
        in_specs=[pl.BlockSpec(memory_space=pl.ANY),
                  pl.BlockSpec(memory_space=pl.ANY),
                  pl.BlockSpec((1, 1, d_model, FC), lambda w, f, ie, inn, ist: (layer, ie[w], 0, f)),
                  pl.BlockSpec((1, 1, d_model, FC), lambda w, f, ie, inn, ist: (layer, ie[w], 0, n_fc + f)),
                  pl.BlockSpec((1, 1, 1, FC), lambda w, f, ie, inn, ist: (layer, ie[w], 0, f)),
                  pl.BlockSpec((1, 1, 1, FC), lambda w, f, ie, inn, ist: (layer, ie[w], 0, n_fc + f)),
                  pl.BlockSpec((1, 1, FC, d_model), lambda w, f, ie, inn, ist: (layer, ie[w], f, 0)),
                  pl.BlockSpec((None, None, 1, d_model), lambda w, f, ie, inn, ist: (layer, ie[w], 0, 0))],
        out_specs=pl.BlockSpec(memory_space=pl.ANY),
        scratch_shapes=[pltpu.SMEM((TOK_WIN, LANES), jnp.int32),
                        pltpu.SMEM((1,), jnp.int32),
                        pltpu.VMEM((CAP, d_model), F32),
                        pltpu.VMEM((CAP, d_model), F32),
                        pltpu.SemaphoreType.DMA(()), pltpu.SemaphoreType.DMA(()), pltpu.SemaphoreType.DMA(())],
    )
    return pl.pallas_call(
        functools.partial(_expert_kernel, n_fc=n_fc, n_items=n_items),
        grid_spec=grid_spec,
        out_shape=jax.ShapeDtypeStruct((n_items, CAP, d_model), F32),
        compiler_params=_cparams(("arbitrary", "arbitrary")),
        name="experts",
    )(item_e, item_n, item_start, tok2d, hn, w_gate_up, w_gate_up, bgu, bgu, w_down, bd)


def _combine_kernel(slot_ref, h1_ref, tw_ref, eo_hbm, out_ref, slot_smem, buf, ssem, gsem):
    cp = pltpu.make_async_copy(slot_ref.at[0, 0], slot_smem, ssem)
    cp.start()
    cp.wait()

    def issue(r, c):
        for k in range(TOP_K):
            pltpu.make_async_copy(eo_hbm.at[pl.ds(slot_smem[r * TOP_K + k], 1), :],
                                  buf.at[k, pl.ds(r, 1), :], gsem).start()
        return c

    lax.fori_loop(0, TC, issue, 0, unroll=2)
    for k in range(TOP_K):
        pltpu.make_async_copy(eo_hbm.at[pl.ds(0, TC), :], buf.at[k], gsem).wait()
    tw = tw_ref[...]
    out = h1_ref[...]
    for k in range(TOP_K):
        out = out + tw[:, k:k + 1] * buf[k]
    out_ref[...] = out


def _combine(slots, h1, tw, eo):
    nf, d_model = h1.shape
    row = lambda w: pl.BlockSpec((TC, w), lambda i: (i, 0))
    return pl.pallas_call(
        _combine_kernel,
        grid=(nf // TC,),
        in_specs=[pl.BlockSpec((1, 1, TC * TOP_K), lambda i: (i, 0, 0)),
                  row(d_model), row(LANES), pl.BlockSpec(memory_space=pl.ANY)],
        out_specs=row(d_model),
        out_shape=jax.ShapeDtypeStruct((nf, d_model), F32),
        scratch_shapes=[pltpu.SMEM((TC * TOP_K,), jnp.int32),
                        pltpu.VMEM((TOP_K, TC, d_model), F32),
                        pltpu.SemaphoreType.DMA(()), pltpu.SemaphoreType.DMA(())],
        compiler_params=_cparams(("arbitrary",)),
        name="combine",
    )(slots, h1, tw, eo)


def _route(ti, valid, n_experts, n_items):
    nf = ti.shape[0]
    n_flat = nf * TOP_K
    e_flat = jnp.where(valid[:, None], ti[:, :TOP_K], n_experts).reshape(-1)
    onehot = (e_flat[:, None] == jnp.arange(n_experts, dtype=jnp.int32)[None, :]).astype(jnp.int32)
    csum = jnp.cumsum(onehot, axis=0)
    counts = csum[-1]
    rank = jnp.sum(onehot * csum, axis=1) - 1
    seg_start = jnp.cumsum(counts) - counts
    items_per = (counts + CAP - 1) // CAP
    item_cum = jnp.cumsum(items_per)
    item_first = item_cum - items_per
    total = item_cum[-1]
    w = jnp.arange(n_items, dtype=jnp.int32)
    e_of_w = jnp.minimum(jnp.searchsorted(item_cum, w, side="right"), n_experts - 1).astype(jnp.int32)
    j = w - item_first[e_of_w]
    active = w < total
    e_last = e_of_w[jnp.maximum(total - 1, 0)]
    item_e = jnp.where(active, e_of_w, e_last).astype(jnp.int32)
    item_n = jnp.where(active, jnp.clip(counts[e_of_w] - j * CAP, 0, CAP), 0).astype(jnp.int32)
    item_start = jnp.where(active, seg_start[e_of_w] + j * CAP, 0).astype(jnp.int32)
    order = jnp.argsort(e_flat, stable=True).astype(jnp.int32)
    n_tok = -(-(n_flat + CAP + TOK_ALIGN) // TOK_ALIGN) * TOK_ALIGN
    tok2d = jnp.pad(order // TOP_K, (0, n_tok - n_flat)).reshape(n_tok // LANES, LANES)
    first_of = jnp.sum(onehot * item_first[None, :], axis=1)
    slot_flat = jnp.where(e_flat < n_experts, (first_of + rank // CAP) * CAP + rank % CAP, 0).astype(jnp.int32)
    return item_e, item_n, item_start, tok2d, slot_flat.reshape(nf // TC, 1, TC * TOP_K)


def _rope_tables(pos):
    half = ROPE // 2
    inv = jnp.power(ROPE_THETA, -jnp.arange(half, dtype=F32) / half)
    ang = pos.astype(F32)[:, None] * inv[None, :]
    return jnp.cos(ang), jnp.sin(ang)


def kernel(x_prompt, x_sample, cache_mla_latent, cache_mla_krope, cache_diff_k, cache_diff_v, page_table,
           meta_tokens, rel_bias, g_attn, w_in, g_qa, w_qb, g_kv, w_kvb, g_mla_q, g_mla_k, g_diff_q, g_diff_k,
           lambda_q1, lambda_k1, lambda_q2, lambda_k2, g_subln, w_o, g_ffn, w_router, b_router,
           w_gate_up, b_gate_up, w_down, b_down):
    batch, seq, d_model = x_prompt.shape
    n_seq, s_len = x_sample.shape[0], x_sample.shape[1]
    depth = g_attn.shape[0]
    q_lora = g_qa.shape[1]
    kv_lora = g_kv.shape[1]
    n_heads = w_qb.shape[2]
    n_dheads = rel_bias.shape[1]
    n_kvheads = cache_diff_k.shape[3]
    per_kv = n_dheads // n_kvheads
    n_experts = w_router.shape[2]
    n_pool = cache_mla_latent.shape[1]
    n_pages = page_table.shape[1]
    past = n_pages * PAGE
    assert depth == 1 and w_qb.shape[3] == QK and cache_mla_krope.shape[3] == ROPE
    assert cache_diff_k.shape[4] == 2 * HALF and cache_diff_v.shape[4] == LANES and w_kvb.shape[3] == 2 * NOPE
    assert s_len == 8 and n_heads == 8 and n_experts <= LANES
    dims = dict(n_heads=n_heads, n_dheads=n_dheads, n_kvheads=n_kvheads, q_lora=q_lora, kv_lora=kv_lora)
    layer = 0
    lam_init = 0.8 - 0.6 * math.exp(-0.3 * layer)
    sub_scale = 1.0 - lam_init
    mla_scale = QK ** -0.5
    diff_scale = HALF ** -0.5

    l_tot = seq + N_META
    lp = -(-l_tot // PAGE) * PAGE
    n_p = batch * lp
    n_s = n_seq * s_len
    assert n_s % LANES == 0 and n_p % TM == 0
    nf = -(-(n_p + n_s) // TM) * TM
    meta = jnp.broadcast_to(meta_tokens.astype(F32)[None], (batch, N_META, d_model))
    h_p = jnp.concatenate([meta, x_prompt, jnp.zeros((batch, lp - l_tot, d_model), F32)], axis=1)
    h_flat = jnp.concatenate([h_p.reshape(n_p, d_model), x_sample.reshape(n_s, d_model),
                              jnp.zeros((nf - n_p - n_s, d_model), F32)], axis=0)
    pos = jnp.concatenate([jnp.tile(jnp.arange(lp, dtype=jnp.int32), batch),
                           jnp.tile(past + jnp.arange(s_len, dtype=jnp.int32), n_seq),
                           jnp.zeros((nf - n_p - n_s,), jnp.int32)])
    cos, sin = _rope_tables(pos)
    z = jnp.zeros_like(cos)
    ct = jnp.concatenate([cos, cos, z, z], axis=1)
    st = jnp.concatenate([-sin, sin, z, z], axis=1)
    row_idx = jnp.arange(nf, dtype=jnp.int32)
    valid = jnp.where(row_idx < n_p, (row_idx % lp) < l_tot, row_idx < n_p + n_s)

    w_in_l = w_in[layer]
    o = np.cumsum([0, q_lora, kv_lora, ROPE, n_dheads * 2 * HALF, n_kvheads * 2 * HALF, n_kvheads * LANES])
    w_in_p = jnp.concatenate([w_in_l[:, o[0]:o[2]], w_in_l[:, o[2]:o[3]], w_in_l[:, o[2]:o[3]],
                              w_in_l[:, o[3]:o[6]]], axis=1).astype(BF16)
    wq = w_qb[layer]
    w_q = jnp.concatenate([wq, wq[:, :, NOPE:]], axis=2).reshape(q_lora, n_heads * 256).astype(BF16)
    gq = g_mla_q[layer] * mla_scale
    g_q = jnp.tile(jnp.concatenate([gq, gq[NOPE:]]), n_heads)[None]
    gk = g_mla_k[layer]
    w_uk3 = w_kvb[layer][:, :, :NOPE]
    w_uv3 = w_kvb[layer][:, :, NOPE:]
    prm = dict(
        g_attn=g_attn[layer][None], w_in=w_in_p, g_qa=g_qa[layer][None], w_q=w_q, g_q=g_q,
        g_kv=g_kv[layer][None], w_uk=w_uk3.reshape(kv_lora, n_heads * NOPE).astype(BF16),
        g_kn=jnp.tile(gk[:NOPE], n_heads)[None], g_kr=jnp.concatenate([gk[NOPE:], gk[NOPE:]])[None],
        g_dq=jnp.tile(g_diff_q[layer] * diff_scale, 2 * n_dheads)[None],
        g_dk=jnp.tile(g_diff_k[layer], 2 * n_kvheads)[None])
    prm["segq"], prm["expq"] = _seg_mats(n_heads * 256, 256, n_heads, skip=lambda i: i % 256 >= QK)
    prm["segk"], prm["expk"] = _seg_mats(n_heads * NOPE, NOPE, n_heads)
    prm["segdq"], prm["expdq"] = _seg_mats(n_dheads * 2 * HALF, HALF, 2 * n_dheads)
    prm["segdk"], prm["expdk"] = _seg_mats(n_kvheads * 2 * HALF, HALF, 2 * n_kvheads)

    c_f, kr_f, kd_f, vd_f, qm, km, ca, qd, kdb, va = _project(h_flat, ct, st, prm, dims)

    si = np.arange(PAGE)[:, None]
    ti_ = np.arange(PAGE)[None, :]
    bk0 = np.where(si >= ti_, _bucket_np(si - ti_), -1)
    bk1 = _bucket_np(PAGE + si - ti_)
    drow = np.arange(2 * s_len * per_kv)
    d_s = (drow // per_kv) % s_len
    d_r = drow % per_kv
    n_drow = drow.shape[0]
    assert n_drow <= PAGE
    bk_last = np.zeros((PAGE, PAGE), np.int32)
    bk_last[:n_drow] = _bucket_np(PAGE + d_s[:, None] - ti_)
    bk_new = np.full((PAGE, PAGE), -1, np.int32)
    bk_new[:n_drow] = np.where((ti_ <= d_s[:, None]) & (ti_ < s_len), _bucket_np(d_s[:, None] - ti_), -1)
    bks = [bk0] * n_dheads + [bk1] * n_dheads + [bk_last] * n_kvheads + [bk_new] * n_kvheads
    rb_t = rel_bias.astype(F32).T
    head_rows = [jnp.broadcast_to(rb_t[h][None], (PAGE, N_BUCKETS)) for h in range(n_dheads)]
    dec_rows = []
    for g in range(n_kvheads):
        sel = np.zeros((PAGE,), np.int32)
        sel[:n_drow] = g * per_kv + d_r
        dec_rows.append(rb_t[jnp.asarray(sel)])
    rts = head_rows + head_rows + dec_rows + dec_rows
    tiles, lam = _bias_prep(jnp.asarray(np.stack(bks)), jnp.stack(rts), lambda_q1[layer][None],
                            lambda_k1[layer][None], lambda_q2[layer][None], lambda_k2[layer][None], lam_init)

    def stack_bias(t):
        t = t.reshape(n_kvheads, per_kv, 1, PAGE, PAGE)
        return jnp.broadcast_to(t, (n_kvheads, per_kv, 2, PAGE, PAGE)).reshape(n_kvheads, per_kv * 2 * PAGE, PAGE)

    b0 = stack_bias(tiles[:n_dheads])
    b1 = stack_bias(tiles[n_dheads:2 * n_dheads])
    tlast = tiles[2 * n_dheads:2 * n_dheads + n_kvheads, :n_drow]
    tnew = tiles[2 * n_dheads + n_kvheads:, :n_drow]

    wuv = jnp.transpose(w_uv3, (1, 0, 2)).astype(BF16)
    gsub = g_subln[layer][None]
    cat = _prompt_attn(qm, qd, km, ca, kdb, va, wuv, b0, b1, gsub, lam, batch, lp, dims, sub_scale)

    wukt3 = jnp.transpose(w_uk3, (1, 2, 0)).astype(BF16)
    perm = np.zeros((LANES, LANES), np.float32)
    for i in range(32):
        perm[i, i] = 1.0
        perm[i, 32 + i] = -1.0
        perm[32 + i, 64 + i] = 1.0
        perm[32 + i, 96 + i] = 1.0
    qabs, qf = _sample_q(qm, prm["g_kn"][:, :NOPE], wukt3, jnp.asarray(perm, BF16), n_p, n_s, n_heads)
    hs = n_heads * s_len
    to_hs = lambda a, w: a.reshape(n_seq, s_len, n_heads, w).transpose(0, 2, 1, 3).reshape(n_seq, hs, w)
    qabs_b = to_hs(qabs, kv_lora)
    qf_b = to_hs(qf, LANES)
    krsq_rows = np.zeros((8, 2 * LANES), np.float32)
    krsq_rows[:, LANES:LANES + ROPE] = 1.0
    lhs2 = jnp.concatenate([jnp.concatenate([qf_b, jnp.zeros_like(qf_b)], axis=2),
                            jnp.broadcast_to(jnp.asarray(krsq_rows, BF16)[None], (n_seq, 8, 2 * LANES))], axis=1)
    qd_s = qd[n_p:n_p + n_s].reshape(n_seq, s_len, n_kvheads, per_kv, 2, LANES)
    qdl = qd_s.transpose(0, 2, 4, 1, 3, 5).reshape(n_seq, n_kvheads, n_drow, LANES)
    seq_rows = lambda a: a[n_p:n_p + n_s].reshape(n_seq, s_len, a.shape[1])
    pages = DECODE_PAGES if n_pages % DECODE_PAGES == 0 else 1
    t_chunk = pages * PAGE
    kc, ks = _rope_tables(jnp.arange(past + PAGE, dtype=jnp.int32))
    cst_all = jnp.concatenate([kc, ks, ks, kc], axis=1).T
    cst = cst_all[:, :past].reshape(LANES, n_pages // pages, t_chunk).transpose(1, 0, 2)
    cst_new = cst_all[:, past:]
    gkr = gk[NOPE:]
    ggt = jnp.broadcast_to(jnp.concatenate([gkr, gkr])[:, None], (LANES, LANES))
    krt_new = jnp.pad(jnp.swapaxes(seq_rows(kr_f), 1, 2), ((0, 0), (0, 0), (0, PAGE - s_len)))
    j_ = np.arange(LANES)[None, :]
    s_of_row = (np.arange(hs) % s_len)[:, None]
    mnew = jnp.asarray(np.where((j_ <= s_of_row) & (j_ < s_len), 0.0, NEG_INF).astype(np.float32))
    krp = jnp.swapaxes(cache_mla_krope, 2, 3)
    dkp = cache_diff_k.reshape(depth, n_pool, PAGE * n_kvheads, 2 * HALF)
    dvp = cache_diff_v.reshape(depth, n_pool, PAGE * n_kvheads, LANES)
    olat, od = _decode_attn(page_table, cache_mla_latent, krp, dkp, dvp, qabs_b, lhs2, qdl,
                            seq_rows(c_f), krt_new, seq_rows(kd_f), seq_rows(vd_f),
                            wukt3.reshape(n_heads * NOPE, kv_lora), cst, cst_new, ggt, tlast, tnew, mnew, lam,
                            layer, dims, pages)
    olat_t = olat.reshape(n_seq, n_heads, s_len, kv_lora).transpose(0, 2, 1, 3).reshape(n_s, n_heads * kv_lora)
    od_t = od.reshape(n_seq, n_kvheads, s_len, per_kv, LANES).transpose(0, 2, 1, 3, 4).reshape(n_s, n_dheads * LANES)
    cat_s = _sample_post(olat_t, od_t, wuv, gsub, nf - n_p, dims, sub_scale)

    wr = jnp.pad(w_router[layer].astype(F32), ((0, 0), (0, LANES - n_experts)))
    wrh, wrl = _split_bf16(wr)
    br = jnp.pad(b_router[layer].astype(F32), (0, LANES - n_experts))[None]
    h1, hn, ti, tw = _merge(cat, cat_s, h_flat, w_o[layer].astype(BF16), g_ffn[layer][None], wrh, wrl, br,
                            n_experts)

    n_valid = batch * l_tot + n_s
    n_items = n_experts + -(-(n_valid * TOP_K) // CAP)
    item_e, item_n, item_start, tok2d, slots = _route(ti, valid, n_experts, n_items)
    eo = _experts(item_e, item_n, item_start, tok2d, hn, w_gate_up, b_gate_up, w_down, b_down, layer)
    out = _combine(slots, h1, tw, eo.reshape(n_items * CAP, d_model))

    y_prompt = out[:n_p].reshape(batch, lp, d_model)[:, N_META:l_tot]
    y_sample = out[n_p:n_p + n_s].reshape(n_seq, s_len, d_model)
    pr = lambda a: a[:n_p].reshape(batch, lp, a.shape[1])[:, :l_tot][None]
    sr = lambda a: a[n_p:n_p + n_s].reshape(n_seq, s_len, a.shape[1])[None]
    kv5 = lambda a: a.reshape(a.shape[:3] + (n_kvheads, a.shape[3] // n_kvheads))
    return (y_prompt, y_sample, pr(c_f), pr(kr_f), kv5(pr(kd_f)), kv5(pr(vd_f)),
            sr(c_f), sr(kr_f), kv5(sr(kd_f)), kv5(sr(vd_f)))
```

```python
import functools
import math

import numpy as np
import jax
import jax.numpy as jnp
from jax import lax
from jax.experimental import pallas as pl
from jax.experimental.pallas import tpu as pltpu

F32 = jnp.float32
BF16 = jnp.bfloat16
NEG_INF = float("-inf")

N_META = 16
ROPE_THETA = 10000.0
N_BUCKETS = 32
MAX_DISTANCE = 128
TOP_K = 4
SWIGLU_LIMIT = 7.0
SWIGLU_ALPHA = 1.702
EPS = 1e-6
PAGE = 128
NOPE = 128
ROPE = 64
QK = NOPE + ROPE
HALF = 64

LANES = 128
TM = 256
VMEM_LIMIT = 56 * 1024 * 1024
DECODE_PAGES = 16

CAP = 2048
RS = 256
FC = 256
TC = 128
TOK_ALIGN = 1024
TOK_WIN = (CAP + TOK_ALIGN) // LANES

NT_DIMS = (((1,), (1,)), ((), ()))


def _cparams(sem):
    return pltpu.CompilerParams(dimension_semantics=sem, vmem_limit_bytes=VMEM_LIMIT)


def _dot(a, b):
    return jnp.dot(a, b, preferred_element_type=F32)


def _dot_nt(a, b):
    return lax.dot_general(a, b, NT_DIMS, preferred_element_type=F32)


def _split_bf16(x):
    hi = x.astype(BF16)
    lo = (x - hi.astype(F32)).astype(BF16)
    return hi, lo


def _seg_rsqrt(sq, seg_ref, exp_ref, width, extra=None):
    ssq = _dot(sq.astype(BF16), seg_ref[...])
    if extra is not None:
        ssq = ssq + extra
    rs = lax.rsqrt(ssq * (1.0 / width) + EPS)
    hi, lo = _split_bf16(rs)
    return _dot(hi, exp_ref[...]) + _dot(lo, exp_ref[...])


def _rms_rows(x, width):
    return lax.rsqrt(jnp.sum(x * x, axis=-1, keepdims=True) * (1.0 / width) + EPS)


def _lane_tile(x, reps):
    return x if reps == 1 else jnp.concatenate([x] * reps, axis=1)


def _bucket_np(n):
    n = np.maximum(n, 0)
    max_exact = N_BUCKETS // 2
    nf = np.maximum(n, 1).astype(np.float32)
    large = max_exact + (np.log(nf / max_exact) / math.log(MAX_DISTANCE / max_exact)
                         * (N_BUCKETS - max_exact)).astype(np.int32)
    large = np.minimum(large, N_BUCKETS - 1)
    return np.where(n < max_exact, n, large).astype(np.int32)


def _prep_kernel(bk_ref, rt_ref, lq1_ref, lk1_ref, lq2_ref, lk2_ref, tile_ref, lam_ref, *, lam_init):
    for t in range(bk_ref.shape[0]):
        bk = bk_ref[t]
        rt = rt_ref[t]
        far = rt[:, N_BUCKETS - 1:N_BUCKETS]
        acc = jnp.zeros((PAGE, LANES), F32)
        for b in range(N_BUCKETS):
            acc = jnp.where(bk == b, rt[:, b:b + 1] - far, acc)
        tile_ref[t] = jnp.where(bk < 0, NEG_INF, acc)
    s1 = jnp.sum(lq1_ref[...] * lk1_ref[...], axis=-1, keepdims=True)
    s2 = jnp.sum(lq2_ref[...] * lk2_ref[...], axis=-1, keepdims=True)
    lam = jnp.exp(s1) - jnp.exp(s2) + lam_init
    lam_ref[...] = jnp.broadcast_to(lam, lam_ref.shape)


def _bias_prep(bk, rt, lq1, lk1, lq2, lk2, lam_init):
    n = bk.shape[0]
    per = 4 if n % 4 == 0 else 1
    vec = pl.BlockSpec((1, HALF), lambda i: (0, 0))
    return pl.pallas_call(
        functools.partial(_prep_kernel, lam_init=lam_init),
        grid=(n // per,),
        in_specs=[pl.BlockSpec((per, PAGE, LANES), lambda i: (i, 0, 0)),
                  pl.BlockSpec((per, PAGE, N_BUCKETS), lambda i: (i, 0, 0)),
                  vec, vec, vec, vec],
        out_specs=[pl.BlockSpec((per, PAGE, LANES), lambda i: (i, 0, 0)),
                   pl.BlockSpec((8, LANES), lambda i: (0, 0))],
        out_shape=[jax.ShapeDtypeStruct((n, PAGE, LANES), F32),
                   jax.ShapeDtypeStruct((8, LANES), F32)],
        compiler_params=_cparams(("arbitrary",)),
        name="bias_prep",
    )(bk, rt, lq1, lk1, lq2, lk2)


def _proj_kernel(x_ref, ct_ref, st_ref, gattn_ref, win_ref, gqa_ref, wq_ref, gq_ref, gkv_ref, wuk_ref,
                 gkn_ref, gkr_ref, gdq_ref, gdk_ref,
                 segq_ref, expq_ref, segk_ref, expk_ref, segdq_ref, expdq_ref, segdk_ref, expdk_ref,
                 c_ref, kr_ref, kd_ref, vd_ref, qm_ref, km_ref, ca_ref, qd_ref, kdb_ref, va_ref,
                 *, d_model, q_lora, kv_lora, n_heads, n_dheads, n_kvheads):
    x = x_ref[...]
    ct = ct_ref[...]
    st = st_ref[...]
    xn = x * _rms_rows(x, d_model) * gattn_ref[...]
    p = _dot(xn.astype(BF16), win_ref[...])
    o_kv = q_lora
    o_kr = o_kv + kv_lora
    o_dq = o_kr + LANES
    o_dk = o_dq + n_dheads * LANES
    o_dv = o_dk + n_kvheads * LANES
    qa = p[:, :q_lora]
    kv = p[:, o_kv:o_kr]
    krd = p[:, o_kr:o_dq]
    dq = p[:, o_dq:o_dk]
    dk = p[:, o_dk:o_dv]
    dv = p[:, o_dv:o_dv + n_kvheads * LANES]

    def rope(v):
        return v * ct + pltpu.roll(v, 32, 1) * st

    qan = qa * _rms_rows(qa, q_lora) * gqa_ref[...]
    q = _dot(qan.astype(BF16), wq_ref[...])
    qg = q * _seg_rsqrt(q * q, segq_ref, expq_ref, QK) * gq_ref[...]
    pieces = []
    for h in range(n_heads):
        pieces.append(qg[:, h * 256:h * 256 + LANES])
        pieces.append(rope(qg[:, h * 256 + LANES:(h + 1) * 256]))
    qm_ref[...] = jnp.concatenate(pieces, axis=1).astype(BF16)

    c = kv * _rms_rows(kv, kv_lora) * gkv_ref[...]
    c_ref[...] = c
    cb = c.astype(BF16)
    ones = jnp.ones((x.shape[0], LANES), BF16)
    ca_ref[...] = jnp.concatenate([cb, ones], axis=1)
    kr = krd[:, :ROPE]
    kr_ref[...] = kr

    kn = _dot(cb, wuk_ref[...])
    kr_ssq = jnp.sum(kr * kr, axis=-1, keepdims=True)
    rk = _seg_rsqrt(kn * kn, segk_ref, expk_ref, QK, extra=kr_ssq)
    kng = kn * rk * gkn_ref[...]
    krr = rope(krd * gkr_ref[...])
    pieces = []
    for h in range(n_heads):
        pieces.append(kng[:, h * LANES:(h + 1) * LANES])
        pieces.append(krr * rk[:, h * LANES:(h + 1) * LANES])
    km_ref[...] = jnp.concatenate(pieces, axis=1).astype(BF16)

    qd = dq * _seg_rsqrt(dq * dq, segdq_ref, expdq_ref, HALF) * gdq_ref[...]
    lane = lax.broadcasted_iota(jnp.int32, (x.shape[0], LANES), 1)
    pieces = []
    for h in range(n_dheads):
        blk = qd[:, h * LANES:(h + 1) * LANES]
        pieces.append(jnp.where(lane < HALF, blk, 0.0))
        pieces.append(jnp.where(lane >= HALF, blk, 0.0))
    qd_ref[...] = jnp.concatenate(pieces, axis=1).astype(BF16)
    kd = dk * _seg_rsqrt(dk * dk, segdk_ref, expdk_ref, HALF) * gdk_ref[...]
    kd_ref[...] = kd
    kdb_ref[...] = kd.astype(BF16)
    vd_ref[...] = dv
    dvb = dv.astype(BF16)
    va_ref[...] = jnp.concatenate([t for g in range(n_kvheads)
                                   for t in (dvb[:, g * LANES:(g + 1) * LANES], ones)], axis=1)


def _seg_mats(width, seg, n_seg, skip=None):
    m = np.zeros((width, LANES), np.float32)
    for i in range(width):
        s = i // seg
        if s < n_seg and not (skip is not None and skip(i)):
            m[i, s] = 1.0
    e = np.zeros((LANES, width), np.float32)
    for i in range(width):
        s = i // seg
        if s < n_seg:
            e[s, i] = 1.0
    return jnp.asarray(m, BF16), jnp.asarray(e, BF16)


def _project(h_flat, ct, st, prm, dims):
    nf, d_model = h_flat.shape
    n_heads, n_dheads, n_kvheads = dims["n_heads"], dims["n_dheads"], dims["n_kvheads"]
    q_lora, kv_lora = dims["q_lora"], dims["kv_lora"]
    row = lambda w: pl.BlockSpec((TM, w), lambda i: (i, 0))
    full = lambda a: pl.BlockSpec(a.shape, lambda i: (0,) * a.ndim)
    consts = [prm["g_attn"], prm["w_in"], prm["g_qa"], prm["w_q"], prm["g_q"], prm["g_kv"], prm["w_uk"],
              prm["g_kn"], prm["g_kr"], prm["g_dq"], prm["g_dk"],
              prm["segq"], prm["expq"], prm["segk"], prm["expk"], prm["segdq"], prm["expdq"],
              prm["segdk"], prm["expdk"]]
    kvw = n_kvheads * LANES
    out_w = [(kv_lora, F32), (ROPE, F32), (kvw, F32), (kvw, F32),
             (n_heads * 256, BF16), (n_heads * 256, BF16), (kv_lora + LANES, BF16),
             (n_dheads * 256, BF16), (kvw, BF16), (2 * kvw, BF16)]
    return pl.pallas_call(
        functools.partial(_proj_kernel, d_model=d_model, q_lora=q_lora, kv_lora=kv_lora,
                          n_heads=n_heads, n_dheads=n_dheads, n_kvheads=n_kvheads),
        grid=(nf // TM,),
        in_specs=[row(d_model), row(LANES), row(LANES)] + [full(a) for a in consts],
        out_specs=[row(w) for w, _ in out_w],
        out_shape=[jax.ShapeDtypeStruct((nf, w), dt) for w, dt in out_w],
        compiler_params=_cparams(("arbitrary",)),
        name="project",
    )(h_flat, ct, st, *consts)


def _sample_q_kernel(qm_ref, gkn_ref, wukt_ref, perm_ref, qabs_ref, qf_ref, *, n_heads):
    for h in range(n_heads):
        qn = qm_ref[:, h * 256:h * 256 + LANES].astype(F32) * gkn_ref[...]
        qabs_ref[:, h * 256:(h + 1) * 256] = _dot(qn.astype(BF16), wukt_ref[h]).astype(BF16)
        qr = qm_ref[:, h * 256 + LANES:(h + 1) * 256]
        qf_ref[:, h * LANES:(h + 1) * LANES] = _dot(qr, perm_ref[...]).astype(BF16)


def _sample_q(qm, gkn, wukt, perm, row0, ns, n_heads):
    blk0 = row0 // LANES
    return pl.pallas_call(
        functools.partial(_sample_q_kernel, n_heads=n_heads),
        grid=(ns // LANES,),
        in_specs=[pl.BlockSpec((LANES, n_heads * 256), lambda i: (blk0 + i, 0)),
                  pl.BlockSpec(gkn.shape, lambda i: (0, 0)),
                  pl.BlockSpec(wukt.shape, lambda i: (0, 0, 0)),
                  pl.BlockSpec(perm.shape, lambda i: (0, 0))],
        out_specs=[pl.BlockSpec((LANES, n_heads * 256), lambda i: (i, 0)),
                   pl.BlockSpec((LANES, n_heads * LANES), lambda i: (i, 0))],
        out_shape=[jax.ShapeDtypeStruct((ns, n_heads * 256), BF16),
                   jax.ShapeDtypeStruct((ns, n_heads * LANES), BF16)],
        compiler_params=_cparams(("arbitrary",)),
        name="sample_q",
    )(qm, gkn, wukt, perm)


def _prompt_attn_kernel(qm_ref, qd_ref, km_ref, ca_ref, kd_ref, va_ref, wuv_ref, b0_ref, b1_ref,
                        gsub_ref, lam_ref, cat_ref, m1, a1, m2, a2, qs_sc,
                        *, n_heads, n_kvheads, per_kv, kv_lora, sub_scale):
    i = pl.program_id(1)
    row = lax.broadcasted_iota(jnp.int32, (PAGE, PAGE), 0)
    col = lax.broadcasted_iota(jnp.int32, (PAGE, PAGE), 1)
    causal = col <= row
    n_stack = per_kv * 2
    va_w = 2 * LANES

    m1[...] = jnp.full(m1.shape, NEG_INF, F32)
    a1[...] = jnp.zeros(a1.shape, F32)
    m2[...] = jnp.full(m2.shape, NEG_INF, F32)
    a2[...] = jnp.zeros(a2.shape, F32)
    for g in range(n_kvheads):
        for t in range(n_stack):
            c0 = (g * n_stack + t) * LANES
            qs_sc[g, t * PAGE:(t + 1) * PAGE, :] = qd_ref[:, c0:c0 + LANES]

    def soft(s, m_ref, idx):
        m_old = m_ref[idx]
        m_new = jnp.maximum(m_old, jnp.max(s, axis=-1, keepdims=True))
        m_ref[idx] = m_new
        return jnp.exp(m_old - m_new), jnp.exp(s - m_new).astype(BF16)

    def block(j, bias_ref, masked):
        r0 = pl.multiple_of(j * PAGE, PAGE)
        scores = []
        for h in range(n_heads):
            s = _dot_nt(qm_ref[:, h * 256:(h + 1) * 256], km_ref[pl.ds(r0, PAGE), h * 256:(h + 1) * 256])
            scores.append(jnp.where(causal, s, NEG_INF) if masked else s)
        for g in range(n_kvheads):
            s = _dot_nt(qs_sc[g], kd_ref[pl.ds(r0, PAGE), g * LANES:(g + 1) * LANES])
            scores.append(s if bias_ref is None else s + bias_ref[g])
        softs = [soft(scores[h], m1, h) for h in range(n_heads)]
        softs += [soft(scores[n_heads + g], m2, g) for g in range(n_kvheads)]
        ca = ca_ref[pl.ds(r0, PAGE), :]
        for h in range(n_heads):
            alpha, p = softs[h]
            a1[h] = _lane_tile(alpha, (kv_lora + LANES) // LANES) * a1[h] + _dot(p, ca)
        for g in range(n_kvheads):
            alpha, p = softs[n_heads + g]
            va = va_ref[pl.ds(r0, PAGE), g * va_w:(g + 1) * va_w]
            a2[g] = _lane_tile(alpha, va_w // LANES) * a2[g] + _dot(p, va)

    def far_body(j, c):
        block(j, None, False)
        return c

    lax.fori_loop(0, jnp.maximum(i - 1, 0), far_body, 0)

    @pl.when(i >= 1)
    def _():
        block(i - 1, b1_ref, False)

    block(i, b0_ref, True)

    for h in range(n_heads):
        acc = a1[h]
        inv = 1.0 / acc[:, kv_lora:kv_lora + LANES]
        olat = acc[:, :kv_lora] * _lane_tile(inv, kv_lora // LANES)
        cat_ref[:, h * LANES:(h + 1) * LANES] = _dot(olat.astype(BF16), wuv_ref[h]).astype(BF16)
    lam = lam_ref[0:1, 0:1]
    for g in range(n_kvheads):
        acc = a2[g]
        o = acc[:, :LANES] * (1.0 / acc[:, LANES:])
        for r in range(per_kv):
            o1 = o[(2 * r) * PAGE:(2 * r + 1) * PAGE]
            o2 = o[(2 * r + 1) * PAGE:(2 * r + 2) * PAGE]
            od = o1 - lam * o2
            odn = od * _rms_rows(od, LANES) * gsub_ref[...] * sub_scale
            c0 = (n_heads + g * per_kv + r) * LANES
            cat_ref[:, c0:c0 + LANES] = odn.astype(BF16)


def _prompt_attn(qm, qd, km, ca, kdb, va, wuv, b0, b1, gsub, lam, batch, lp, dims, sub_scale):
    n_heads, n_dheads, n_kvheads = dims["n_heads"], dims["n_dheads"], dims["n_kvheads"]
    kv_lora = dims["kv_lora"]
    per_kv = n_dheads // n_kvheads
    nqb = lp // PAGE
    qrow = lambda w: pl.BlockSpec((PAGE, w), lambda b, i: (b * nqb + i, 0))
    kvrow = lambda w: pl.BlockSpec((lp, w), lambda b, i: (b, 0))
    full = lambda a: pl.BlockSpec(a.shape, lambda b, i: (0,) * a.ndim)
    cat_w = (n_heads + n_dheads) * LANES
    n_stack = per_kv * 2 * PAGE
    return pl.pallas_call(
        functools.partial(_prompt_attn_kernel, n_heads=n_heads, n_kvheads=n_kvheads, per_kv=per_kv,
                          kv_lora=kv_lora, sub_scale=sub_scale),
        grid=(batch, nqb),
        in_specs=[qrow(n_heads * 256), qrow(n_dheads * 256), kvrow(n_heads * 256), kvrow(ca.shape[1]),
                  kvrow(n_kvheads * LANES), kvrow(va.shape[1]),
                  full(wuv), full(b0), full(b1), full(gsub), full(lam)],
        out_specs=qrow(cat_w),
        out_shape=jax.ShapeDtypeStruct((batch * lp, cat_w), BF16),
        scratch_shapes=[pltpu.VMEM((n_heads, PAGE, LANES), F32),
                        pltpu.VMEM((n_heads, PAGE, kv_lora + LANES), F32),
                        pltpu.VMEM((n_kvheads, n_stack, LANES), F32),
                        pltpu.VMEM((n_kvheads, n_stack, 2 * LANES), F32),
                        pltpu.VMEM((n_kvheads, n_stack, LANES), BF16)],
        compiler_params=_cparams(("arbitrary", "arbitrary")),
        name="prompt_attn",
    )(qm, qd, km, ca, kdb, va, wuv, b0, b1, gsub, lam)


def _decode_kernel(pt_ref, lat_hbm, krp_hbm, dk_hbm, dv_hbm,
                   qabs_ref, lhs2_ref, qdl_ref, cnew_ref, krnew_ref, kdnew_ref, vdnew_ref,
                   wukt_ref, cst_ref, cstn_ref, ggt_ref, tlast_ref, tnew_ref, mnew_ref, lam_ref,
                   olat_ref, od_ref,
                   cbuf, rbuf, kbuf, vbuf, sem, lhs1,
                   *, layer, n_seq, n_chunks, pages, n_heads, n_kvheads, s_len, kv_lora, ds_rows):
    b = pl.program_id(0)
    t_chunk = pages * PAGE
    hs = n_heads * s_len
    n_nope = n_heads * NOPE

    def page_copies(seq, chunk, slot):
        cps = []
        for p in range(pages):
            pg = pt_ref[seq, chunk * pages + p]
            cps.append(pltpu.make_async_copy(lat_hbm.at[layer, pg],
                                             cbuf.at[slot, pl.ds(p * PAGE, PAGE), :], sem.at[slot]))
            cps.append(pltpu.make_async_copy(krp_hbm.at[layer, pg],
                                             rbuf.at[slot, :, pl.ds(p * PAGE, PAGE)], sem.at[slot]))
            cps.append(pltpu.make_async_copy(dk_hbm.at[layer, pg],
                                             kbuf.at[slot, pl.ds(p * n_kvheads * PAGE, n_kvheads * PAGE), :],
                                             sem.at[slot]))
            cps.append(pltpu.make_async_copy(dv_hbm.at[layer, pg],
                                             vbuf.at[slot, pl.ds(p * n_kvheads * PAGE, n_kvheads * PAGE), :],
                                             sem.at[slot]))
        return cps

    def start_chunk(seq, chunk, slot):
        for n, cp in enumerate(page_copies(seq, chunk, slot)):
            cp.start(priority=n % 2)

    def wait_chunk(seq, chunk, slot):
        for cp in page_copies(seq, chunk, slot):
            cp.wait()

    @pl.when(b == 0)
    def _():
        start_chunk(0, 0, 0)
        lhs1[0:n_nope, :] = wukt_ref[...]

    lhs1[n_nope:n_nope + hs, :] = qabs_ref[0]

    def fresh(rows, width):
        return (jnp.full((rows, LANES), NEG_INF, F32), jnp.zeros((rows, LANES), F32),
                jnp.zeros((rows, width), F32))

    state0 = (fresh(hs, kv_lora),) + tuple(fresh(ds_rows, LANES) for _ in range(n_kvheads))

    lhs2 = lhs2_ref[0]
    ggt = ggt_ref[...]

    def lane_fold(p):
        acc = p[:, 0:LANES]
        for k in range(1, p.shape[1] // LANES):
            acc = acc + p[:, k * LANES:(k + 1) * LANES]
        return acc

    def key_products(c32, krt, cst):
        reps = c32.shape[0] // LANES
        kall = _dot_nt(lhs1[...], c32.astype(BF16))
        xxt = jnp.concatenate([krt, krt], axis=0)
        f2t = jnp.concatenate([xxt * _lane_tile(ggt, reps) * cst, xxt * xxt], axis=0).astype(BF16)
        return kall, _dot(lhs2, f2t)

    def key_norms(kall, s2):
        krsq = s2[hs:hs + 8]
        rows = []
        for h in range(n_heads):
            kt = kall[h * NOPE:(h + 1) * NOPE]
            nsq = jnp.sum(kt * kt, axis=0, keepdims=True)
            rinv = lax.rsqrt((nsq + krsq) * (1.0 / QK) + EPS)
            r0 = n_nope + h * s_len
            rows.append((kall[r0:r0 + s_len] + s2[h * s_len:(h + 1) * s_len]) * rinv)
        return jnp.concatenate(rows, axis=0)

    def diff_scores(kd_g, diff_bias):
        out = []
        for g in range(n_kvheads):
            sg = _dot_nt(qdl_ref[0, g], kd_g[g].astype(BF16))
            out.append(sg if diff_bias is None else sg + diff_bias[g])
        return out

    def soft(scores, state):
        out = []
        for s, (m_old, l_old, _) in zip(scores, state):
            m_new = jnp.maximum(m_old, jnp.max(s, axis=-1, keepdims=True))
            alpha = jnp.exp(m_old - m_new)
            p = jnp.exp(s - _lane_tile(m_new, s.shape[1] // LANES))
            out.append((m_new, alpha * l_old + lane_fold(p), alpha, p.astype(BF16)))
        return out

    def value_products(softs, state, values):
        new = []
        for (m_new, l_new, alpha, p), (_, _, a_old), v in zip(softs, state, values):
            new.append((m_new, l_new, _lane_tile(alpha, a_old.shape[1] // LANES) * a_old + _dot(p, v)))
        return tuple(new)

    def slot_of(chunk):
        return lax.rem(b * n_chunks + chunk, 3)

    def fetch(chunk):
        wait_chunk(b, chunk, slot_of(chunk))
        last = chunk + 1 == n_chunks
        nseq = jnp.where(last, jnp.minimum(b + 1, n_seq - 1), b)
        start_chunk(nseq, jnp.where(last, 0, chunk + 1), slot_of(chunk + 1))

    def chunk_keys(chunk):
        slot = slot_of(chunk)
        return key_products(cbuf[slot], rbuf[slot], cst_ref[chunk])

    def strided(buf, slot):
        return [buf[slot, pl.ds(g, t_chunk, stride=n_kvheads), :] for g in range(n_kvheads)]

    def chunk_values(chunk):
        slot = slot_of(chunk)
        return [cbuf[slot].astype(BF16)] + [v.astype(BF16) for v in strided(vbuf, slot)]

    def body(chunk, carry):
        s_prev, state = carry
        fetch(chunk)
        sd = diff_scores(strided(kbuf, slot_of(chunk - 1)), None)
        kall, s2 = chunk_keys(chunk)
        softs = soft([s_prev] + sd, state)
        state = value_products(softs, state, chunk_values(chunk - 1))
        return key_norms(kall, s2), state

    fetch(0)
    s_last, state = lax.fori_loop(1, n_chunks, body, (key_norms(*chunk_keys(0)), state0))

    zeros = jnp.zeros((ds_rows, t_chunk - PAGE), F32)
    bias = [jnp.concatenate([zeros, tlast_ref[g]], axis=1) if t_chunk > PAGE else tlast_ref[g]
            for g in range(n_kvheads)]
    sd = diff_scores(strided(kbuf, slot_of(n_chunks - 1)), bias)
    pad = PAGE - s_len
    padr = lambda x: jnp.concatenate([x, jnp.zeros((pad, x.shape[1]), x.dtype)], axis=0)
    c_new = padr(cnew_ref[0])
    kd_new = padr(kdnew_ref[0])
    vd_new = padr(vdnew_ref[0])
    kall, s2 = key_products(c_new, krnew_ref[0], cstn_ref[...])
    state = value_products(soft([s_last] + sd, state), state, chunk_values(n_chunks - 1))
    sd = diff_scores([kd_new[:, g * LANES:(g + 1) * LANES] for g in range(n_kvheads)],
                     [tnew_ref[g] for g in range(n_kvheads)])
    s_new = key_norms(kall, s2) + mnew_ref[...]
    values = [c_new.astype(BF16)] + [vd_new[:, g * LANES:(g + 1) * LANES].astype(BF16) for g in range(n_kvheads)]
    state = value_products(soft([s_new] + sd, state), state, values)

    @pl.when(b == n_seq - 1)
    def _():
        wait_chunk(b, 0, slot_of(n_chunks))

    _, l1, a1 = state[0]
    olat_ref[0] = a1 * (1.0 / jnp.sum(l1, axis=-1, keepdims=True))
    lam = lam_ref[0:1, 0:1]
    half = ds_rows // 2
    for g in range(n_kvheads):
        _, l2, a2 = state[1 + g]
        o = a2 * (1.0 / jnp.sum(l2, axis=-1, keepdims=True))
        od_ref[0, g] = o[:half] - lam * o[half:]


def _decode_attn(page_table, lat, krp, dkp, dvp, qabs, lhs2, qdl, c_new, krt_new, kd_new, vd_new,
                 wukt, cst, cst_new, ggt, tlast, tnew, mnew, lam, layer, dims, pages):
    n_seq, n_pages = page_table.shape
    n_heads, n_kvheads = dims["n_heads"], dims["n_kvheads"]
    s_len = c_new.shape[1]
    kv_lora = c_new.shape[2]
    n_chunks = n_pages // pages
    t_chunk = pages * PAGE
    hs = n_heads * s_len
    ds_rows = qdl.shape[2]
    seq3 = lambda a: pl.BlockSpec((1,) + a.shape[1:], lambda b, pt: (b,) + (0,) * (a.ndim - 1))
    full = lambda a: pl.BlockSpec(a.shape, lambda b, pt: (0,) * a.ndim)
    hbm = pl.BlockSpec(memory_space=pl.ANY)
    grid_spec = pltpu.PrefetchScalarGridSpec(
        num_scalar_prefetch=1,
        grid=(n_seq,),
        in_specs=[hbm, hbm, hbm, hbm,
                  seq3(qabs), seq3(lhs2), seq3(qdl), seq3(c_new), seq3(krt_new), seq3(kd_new), seq3(vd_new),
                  full(wukt), full(cst), full(cst_new), full(ggt), full(tlast), full(tnew), full(mnew), full(lam)],
        out_specs=[pl.BlockSpec((1, hs, kv_lora), lambda b, pt: (b, 0, 0)),
                   pl.BlockSpec((1, n_kvheads, ds_rows // 2, LANES), lambda b, pt: (b, 0, 0, 0))],
        scratch_shapes=[pltpu.VMEM((3, t_chunk, kv_lora), F32),
                        pltpu.VMEM((3, ROPE, t_chunk), F32),
                        pltpu.VMEM((3, n_kvheads * t_chunk, LANES), F32),
                        pltpu.VMEM((3, n_kvheads * t_chunk, LANES), F32),
                        pltpu.SemaphoreType.DMA((3,)),
                        pltpu.VMEM((n_heads * NOPE + hs, kv_lora), BF16)],
    )
    return pl.pallas_call(
        functools.partial(_decode_kernel, layer=layer, n_seq=n_seq, n_chunks=n_chunks, pages=pages,
                          n_heads=n_heads, n_kvheads=n_kvheads, s_len=s_len, kv_lora=kv_lora,
                          ds_rows=ds_rows),
        grid_spec=grid_spec,
        out_shape=[jax.ShapeDtypeStruct((n_seq, hs, kv_lora), F32),
                   jax.ShapeDtypeStruct((n_seq, n_kvheads, ds_rows // 2, LANES), F32)],
        compiler_params=_cparams(("arbitrary",)),
        name="decode_attn",
    )(page_table, lat, krp, dkp, dvp, qabs, lhs2, qdl, c_new, krt_new, kd_new, vd_new,
      wukt, cst, cst_new, ggt, tlast, tnew, mnew, lam)


def _sample_post_kernel(olat_ref, od_ref, wuv_ref, gsub_ref, cat_ref, *, n_heads, n_dheads, kv_lora, sub_scale):
    for h in range(n_heads):
        o = olat_ref[:, h * kv_lora:(h + 1) * kv_lora].astype(BF16)
        cat_ref[:, h * LANES:(h + 1) * LANES] = _dot(o, wuv_ref[h]).astype(BF16)
    for h in range(n_dheads):
        od = od_ref[:, h * LANES:(h + 1) * LANES]
        odn = od * _rms_rows(od, LANES) * gsub_ref[...] * sub_scale
        cat_ref[:, (n_heads + h) * LANES:(n_heads + h + 1) * LANES] = odn.astype(BF16)


def _sample_post(olat, od, wuv, gsub, n_rows, dims, sub_scale):
    ns = olat.shape[0]
    n_heads, n_dheads = dims["n_heads"], dims["n_dheads"]
    kv_lora = dims["kv_lora"]
    cat_w = (n_heads + n_dheads) * LANES
    return pl.pallas_call(
        functools.partial(_sample_post_kernel, n_heads=n_heads, n_dheads=n_dheads, kv_lora=kv_lora,
                          sub_scale=sub_scale),
        grid=(ns // LANES,),
        in_specs=[pl.BlockSpec((LANES, olat.shape[1]), lambda i: (i, 0)),
                  pl.BlockSpec((LANES, od.shape[1]), lambda i: (i, 0)),
                  pl.BlockSpec(wuv.shape, lambda i: (0, 0, 0)),
                  pl.BlockSpec(gsub.shape, lambda i: (0, 0))],
        out_specs=pl.BlockSpec((LANES, cat_w), lambda i: (i, 0)),
        out_shape=jax.ShapeDtypeStruct((n_rows, cat_w), BF16),
        compiler_params=_cparams(("arbitrary",)),
        name="sample_post",
    )(olat, od, wuv, gsub)


def _merge_kernel(catp_ref, cats_ref, h_ref, wo_ref, gffn_ref, wrh_ref, wrl_ref, br_ref,
                  h1_ref, hn_ref, ti_ref, tw_ref, *, d_model, n_experts, n_pblocks):
    cat = jnp.where(pl.program_id(0) < n_pblocks, catp_ref[...], cats_ref[...])
    h1 = h_ref[...] + _dot(cat, wo_ref[...])
    h1_ref[...] = h1
    hn = h1 * _rms_rows(h1, d_model) * gffn_ref[...]
    hn_ref[...] = hn
    hi, lo = _split_bf16(hn)
    logits = _dot(hi, wrh_ref[...]) + _dot(lo, wrh_ref[...]) + _dot(hi, wrl_ref[...]) + br_ref[...]
    lane = lax.broadcasted_iota(jnp.int32, logits.shape, 1)
    lane_f = lane.astype(F32)
    x = jnp.where(lane < n_experts, logits, NEG_INF)
    vals, idxs = [], []
    for _ in range(TOP_K):
        mk = jnp.max(x, axis=-1, keepdims=True)
        ik = jnp.min(jnp.where(x == mk, lane_f, float(LANES)), axis=-1, keepdims=True).astype(jnp.int32)
        vals.append(mk)
        idxs.append(ik)
        x = jnp.where(lane == ik, NEG_INF, x)
    es = [jnp.exp(v - vals[0]) for v in vals]
    den = es[0]
    for e in es[1:]:
        den = den + e
    inv = 1.0 / den
    ti = jnp.zeros(logits.shape, jnp.int32)
    tw = jnp.zeros(logits.shape, F32)
    for k in range(TOP_K):
        ti = jnp.where(lane == k, idxs[k], ti)
        tw = jnp.where(lane == k, es[k] * inv, tw)
    ti_ref[...] = ti
    tw_ref[...] = tw


def _merge(cat_p, cat_s, h_flat, wo, gffn, wrh, wrl, br, n_experts):
    nf, d_model = h_flat.shape
    n_pblocks = cat_p.shape[0] // TM
    n_sblocks = cat_s.shape[0] // TM
    row = lambda w: pl.BlockSpec((TM, w), lambda i: (i, 0))
    full = lambda a: pl.BlockSpec(a.shape, lambda i: (0,) * a.ndim)
    return pl.pallas_call(
        functools.partial(_merge_kernel, d_model=d_model, n_experts=n_experts, n_pblocks=n_pblocks),
        grid=(nf // TM,),
        in_specs=[pl.BlockSpec((TM, cat_p.shape[1]), lambda i: (jnp.minimum(i, n_pblocks - 1), 0)),
                  pl.BlockSpec((TM, cat_s.shape[1]),
                               lambda i: (jnp.clip(i - n_pblocks, 0, n_sblocks - 1), 0)),
                  row(d_model), full(wo), full(gffn), full(wrh), full(wrl), full(br)],
        out_specs=[row(d_model), row(d_model), row(LANES), row(LANES)],
        out_shape=[jax.ShapeDtypeStruct((nf, d_model), F32), jax.ShapeDtypeStruct((nf, d_model), F32),
                   jax.ShapeDtypeStruct((nf, LANES), jnp.int32), jax.ShapeDtypeStruct((nf, LANES), F32)],
        compiler_params=_cparams(("arbitrary",)),
        name="merge",
    )(cat_p, cat_s, h_flat, wo, gffn, wrh, wrl, br)


def _expert_kernel(ie_ref, in_ref, is_ref, tok_hbm, hn_hbm, wg_ref, wu_ref, bg_ref, bu_ref, wd_ref, bd_ref,
                   out_hbm, tok_smem, pend, xs, acc, gsem, tsem, osem, *, n_fc, n_items):
    w = pl.program_id(0)
    f = pl.program_id(1)
    n = in_ref[w]
    n_sub = (n + RS - 1) // RS

    def out_copy(item, t):
        r0 = pl.multiple_of(t * RS, RS)
        return pltpu.make_async_copy(acc.at[pl.ds(r0, RS), :], out_hbm.at[item, pl.ds(r0, RS), :], osem)

    def wait_pending():
        def wbody(t, c):
            out_copy(0, t).wait()
            return c

        lax.fori_loop(0, pend[0], wbody, 0)
        pend[0] = 0

    @pl.when(jnp.logical_and(w == 0, f == 0))
    def _():
        pend[0] = 0

    @pl.when(jnp.logical_and(f == 0, n > 0))
    def _():
        start = is_ref[w]
        win0 = pl.multiple_of((start // TOK_ALIGN) * (TOK_ALIGN // LANES), TOK_ALIGN // LANES)
        off = start % TOK_ALIGN
        cp = pltpu.make_async_copy(tok_hbm.at[pl.ds(win0, TOK_WIN), :], tok_smem, tsem)
        cp.start()
        cp.wait()

        def issue(r8, c):
            for u in range(8):
                r = r8 * 8 + u
                pos = off + r
                tok = tok_smem[lax.shift_right_logical(pos, 7), lax.bitwise_and(pos, LANES - 1)]
                pltpu.make_async_copy(hn_hbm.at[pl.ds(tok, 1), :], xs.at[pl.ds(r, 1), :], gsem).start(priority=1)
            return c

        lax.fori_loop(0, n_sub * (RS // 8), issue, 0)
        wait_pending()

        def prep(t, c):
            r0 = pl.multiple_of(t * RS, RS)
            acc[pl.ds(r0, RS), :] = jnp.zeros((RS, acc.shape[1]), F32)
            pltpu.make_async_copy(hn_hbm.at[pl.ds(0, RS), :], xs.at[pl.ds(r0, RS), :], gsem).wait()
            return c

        lax.fori_loop(0, n_sub, prep, 0)

    @pl.when(n > 0)
    def _():
        wg = wg_ref[0, 0].astype(BF16)
        wu = wu_ref[0, 0].astype(BF16)
        wd = wd_ref[0, 0].astype(BF16)
        bg = bg_ref[0, 0]
        bu = bu_ref[0, 0]

        def sub(t, c):
            r0 = pl.multiple_of(t * RS, RS)
            x = xs[pl.ds(r0, RS), :].astype(BF16)
            gate = jnp.minimum(_dot(x, wg) + bg, SWIGLU_LIMIT)
            up = jnp.clip(_dot(x, wu) + bu, -SWIGLU_LIMIT, SWIGLU_LIMIT)
            act = (up + 1.0) * gate * jax.nn.sigmoid(SWIGLU_ALPHA * gate)
            acc[pl.ds(r0, RS), :] += _dot(act.astype(BF16), wd)
            return c

        lax.fori_loop(0, n_sub, sub, 0)

    @pl.when(jnp.logical_and(f == n_fc - 1, n > 0))
    def _():
        def emit(t, c):
            r0 = pl.multiple_of(t * RS, RS)
            acc[pl.ds(r0, RS), :] = acc[pl.ds(r0, RS), :] + bd_ref[...]
            out_copy(w, t).start()
            return c

        lax.fori_loop(0, n_sub, emit, 0)
        pend[0] = n_sub

    @pl.when(jnp.logical_and(w == n_items - 1, f == n_fc - 1))
    def _():
        wait_pending()


def _experts(item_e, item_n, item_start, tok2d, hn, w_gate_up, b_gate_up, w_down, b_down, layer):
    n_items = item_e.shape[0]
    d_model = hn.shape[1]
    d_ff = w_down.shape[2]
    n_fc = d_ff // FC
    bgu = b_gate_up.reshape(b_gate_up.shape[0], b_gate_up.shape[1], 1, 2 * d_ff)
    bd = b_down.reshape(b_down.shape[0], b_down.shape[1], 1, d_model)
    grid_spec = pltpu.PrefetchScalarGridSpec(
        num_scalar_prefetch=3,
        grid=(n_items, n_fc),
        in_specs=[pl.BlockSpec(memory_space=pl.ANY),
                  pl.BlockSpec(memory_space=pl.ANY),
                  pl.BlockSpec((1, 1, d_model, FC), lambda w, f, ie, inn, ist: (layer, ie[w], 0, f)),
                  pl.BlockSpec((1, 1, d_model, FC), lambda w, f, ie, inn, ist: (layer, ie[w], 0, n_fc + f)),
                  pl.BlockSpec((1, 1, 1, FC), lambda w, f, ie, inn, ist: (layer, ie[w], 0, f)),
                  pl.BlockSpec((1, 1, 1, FC), lambda w, f, ie, inn, ist: (layer, ie[w], 0, n_fc + f)),
                  pl.BlockSpec((1, 1, FC, d_model), lambda w, f, ie, inn, ist: (layer, ie[w], f, 0)),
                  pl.BlockSpec((None, None, 1, d_model), lambda w, f, ie, inn, ist: (layer, ie[w], 0, 0))],
        out_specs=pl.BlockSpec(memory_space=pl.ANY),
        scratch_shapes=[pltpu.SMEM((TOK_WIN, LANES), jnp.int32),
                        pltpu.SMEM((1,), jnp.int32),
                        pltpu.VMEM((CAP, d_model), F32),
                        pltpu.VMEM((CAP, d_model), F32),
                        pltpu.SemaphoreType.DMA(()), pltpu.SemaphoreType.DMA(()), pltpu.SemaphoreType.DMA(())],
    )
    return pl.pallas_call(
        functools.partial(_expert_kernel, n_fc=n_fc, n_items=n_items),
        grid_spec=grid_spec,
        out_shape=jax.ShapeDtypeStruct((n_items, CAP, d_model), F32),
        compiler_params=_cparams(("arbitrary", "arbitrary")),
        name="experts",
    )(item_e, item_n, item_start, tok2d, hn, w_gate_up, w_gate_up, bgu, bgu, w_down, bd)


def _combine_kernel(slot_ref, h1_ref, tw_ref, eo_hbm, out_ref, slot_smem, buf, ssem, gsem):
    cp = pltpu.make_async_copy(slot_ref.at[0, 0], slot_smem, ssem)
    cp.start()
    cp.wait()

    def issue(r, c):
        for k in range(TOP_K):
            pltpu.make_async_copy(eo_hbm.at[pl.ds(slot_smem[r * TOP_K + k], 1), :],
                                  buf.at[k, pl.ds(r, 1), :], gsem).start(priority=k % 2)
        return c

    lax.fori_loop(0, TC, issue, 0, unroll=2)
    for k in range(TOP_K):
        pltpu.make_async_copy(eo_hbm.at[pl.ds(0, TC), :], buf.at[k], gsem).wait()
    tw = tw_ref[...]
    out = h1_ref[...]
    for k in range(TOP_K):
        out = out + tw[:, k:k + 1] * buf[k]
    out_ref[...] = out


def _combine(slots, h1, tw, eo):
    nf, d_model = h1.shape
    row = lambda w: pl.BlockSpec((TC, w), lambda i: (i, 0))
    return pl.pallas_call(
        _combine_kernel,
        grid=(nf // TC,),
        in_specs=[pl.BlockSpec((1, 1, TC * TOP_K), lambda i: (i, 0, 0)),
                  row(d_model), row(LANES), pl.BlockSpec(memory_space=pl.ANY)],
        out_specs=row(d_model),
        out_shape=jax.ShapeDtypeStruct((nf, d_model), F32),
        scratch_shapes=[pltpu.SMEM((TC * TOP_K,), jnp.int32),
                        pltpu.VMEM((TOP_K, TC, d_model), F32),
                        pltpu.SemaphoreType.DMA(()), pltpu.SemaphoreType.DMA(())],
        compiler_params=_cparams(("arbitrary",)),
        name="combine",
    )(slots, h1, tw, eo)


def _route(ti, valid, n_experts, n_items):
    nf = ti.shape[0]
    n_flat = nf * TOP_K
    e_flat = jnp.where(valid[:, None], ti[:, :TOP_K], n_experts).reshape(-1)
    onehot = (e_flat[:, None] == jnp.arange(n_experts, dtype=jnp.int32)[None, :]).astype(jnp.int32)
    csum = jnp.cumsum(onehot, axis=0)
    counts = csum[-1]
    rank = jnp.sum(onehot * csum, axis=1) - 1
    seg_start = jnp.cumsum(counts) - counts
    items_per = (counts + CAP - 1) // CAP
    item_cum = jnp.cumsum(items_per)
    item_first = item_cum - items_per
    total = item_cum[-1]
    w = jnp.arange(n_items, dtype=jnp.int32)
    e_of_w = jnp.minimum(jnp.searchsorted(item_cum, w, side="right"), n_experts - 1).astype(jnp.int32)
    j = w - item_first[e_of_w]
    active = w < total
    e_last = e_of_w[jnp.maximum(total - 1, 0)]
    item_e = jnp.where(active, e_of_w, e_last).astype(jnp.int32)
    item_n = jnp.where(active, jnp.clip(counts[e_of_w] - j * CAP, 0, CAP), 0).astype(jnp.int32)
    item_start = jnp.where(active, seg_start[e_of_w] + j * CAP, 0).astype(jnp.int32)
    order = jnp.argsort(e_flat, stable=True).astype(jnp.int32)
    n_tok = -(-(n_flat + CAP + TOK_ALIGN) // TOK_ALIGN) * TOK_ALIGN
    tok2d = jnp.pad(order // TOP_K, (0, n_tok - n_flat)).reshape(n_tok // LANES, LANES)
    first_of = jnp.sum(onehot * item_first[None, :], axis=1)
    slot_flat = jnp.where(e_flat < n_experts, (first_of + rank // CAP) * CAP + rank % CAP, 0).astype(jnp.int32)
    return item_e, item_n, item_start, tok2d, slot_flat.reshape(nf // TC, 1, TC * TOP_K)


def _rope_tables(pos):
    half = ROPE // 2
    inv = jnp.power(ROPE_THETA, -jnp.arange(half, dtype=F32) / half)
    ang = pos.astype(F32)[:, None] * inv[None, :]
    return jnp.cos(ang), jnp.sin(ang)


def kernel(x_prompt, x_sample, cache_mla_latent, cache_mla_krope, cache_diff_k, cache_diff_v, page_table,
           meta_tokens, rel_bias, g_attn, w_in, g_qa, w_qb, g_kv, w_kvb, g_mla_q, g_mla_k, g_diff_q, g_diff_k,
           lambda_q1, lambda_k1, lambda_q2, lambda_k2, g_subln, w_o, g_ffn, w_router, b_router,
           w_gate_up, b_gate_up, w_down, b_down):
    batch, seq, d_model = x_prompt.shape
    n_seq, s_len = x_sample.shape[0], x_sample.shape[1]
    depth = g_attn.shape[0]
    q_lora = g_qa.shape[1]
    kv_lora = g_kv.shape[1]
    n_heads = w_qb.shape[2]
    n_dheads = rel_bias.shape[1]
    n_kvheads = cache_diff_k.shape[3]
    per_kv = n_dheads // n_kvheads
    n_experts = w_router.shape[2]
    n_pool = cache_mla_latent.shape[1]
    n_pages = page_table.shape[1]
    past = n_pages * PAGE
    assert depth == 1 and w_qb.shape[3] == QK and cache_mla_krope.shape[3] == ROPE
    assert cache_diff_k.shape[4] == 2 * HALF and cache_diff_v.shape[4] == LANES and w_kvb.shape[3] == 2 * NOPE
    assert s_len == 8 and n_heads == 8 and n_experts <= LANES
    dims = dict(n_heads=n_heads, n_dheads=n_dheads, n_kvheads=n_kvheads, q_lora=q_lora, kv_lora=kv_lora)
    layer = 0
    lam_init = 0.8 - 0.6 * math.exp(-0.3 * layer)
    sub_scale = 1.0 - lam_init
    mla_scale = QK ** -0.5
    diff_scale = HALF ** -0.5

    l_tot = seq + N_META
    lp = -(-l_tot // PAGE) * PAGE
    n_p = batch * lp
    n_s = n_seq * s_len
    assert n_s % LANES == 0 and n_p % TM == 0
    nf = -(-(n_p + n_s) // TM) * TM
    meta = jnp.broadcast_to(meta_tokens.astype(F32)[None], (batch, N_META, d_model))
    h_p = jnp.concatenate([meta, x_prompt, jnp.zeros((batch, lp - l_tot, d_model), F32)], axis=1)
    h_flat = jnp.concatenate([h_p.reshape(n_p, d_model), x_sample.reshape(n_s, d_model),
                              jnp.zeros((nf - n_p - n_s, d_model), F32)], axis=0)
    pos = jnp.concatenate([jnp.tile(jnp.arange(lp, dtype=jnp.int32), batch),
                           jnp.tile(past + jnp.arange(s_len, dtype=jnp.int32), n_seq),
                           jnp.zeros((nf - n_p - n_s,), jnp.int32)])
    cos, sin = _rope_tables(pos)
    z = jnp.zeros_like(cos)
    ct = jnp.concatenate([cos, cos, z, z], axis=1)
    st = jnp.concatenate([-sin, sin, z, z], axis=1)
    row_idx = jnp.arange(nf, dtype=jnp.int32)
    valid = jnp.where(row_idx < n_p, (row_idx % lp) < l_tot, row_idx < n_p + n_s)

    w_in_l = w_in[layer]
    o = np.cumsum([0, q_lora, kv_lora, ROPE, n_dheads * 2 * HALF, n_kvheads * 2 * HALF, n_kvheads * LANES])
    w_in_p = jnp.concatenate([w_in_l[:, o[0]:o[2]], w_in_l[:, o[2]:o[3]], w_in_l[:, o[2]:o[3]],
                              w_in_l[:, o[3]:o[6]]], axis=1).astype(BF16)
    wq = w_qb[layer]
    w_q = jnp.concatenate([wq, wq[:, :, NOPE:]], axis=2).reshape(q_lora, n_heads * 256).astype(BF16)
    gq = g_mla_q[layer] * mla_scale
    g_q = jnp.tile(jnp.concatenate([gq, gq[NOPE:]]), n_heads)[None]
    gk = g_mla_k[layer]
    w_uk3 = w_kvb[layer][:, :, :NOPE]
    w_uv3 = w_kvb[layer][:, :, NOPE:]
    prm = dict(
        g_attn=g_attn[layer][None], w_in=w_in_p, g_qa=g_qa[layer][None], w_q=w_q, g_q=g_q,
        g_kv=g_kv[layer][None], w_uk=w_uk3.reshape(kv_lora, n_heads * NOPE).astype(BF16),
        g_kn=jnp.tile(gk[:NOPE], n_heads)[None], g_kr=jnp.concatenate([gk[NOPE:], gk[NOPE:]])[None],
        g_dq=jnp.tile(g_diff_q[layer] * diff_scale, 2 * n_dheads)[None],
        g_dk=jnp.tile(g_diff_k[layer], 2 * n_kvheads)[None])
    prm["segq"], prm["expq"] = _seg_mats(n_heads * 256, 256, n_heads, skip=lambda i: i % 256 >= QK)
    prm["segk"], prm["expk"] = _seg_mats(n_heads * NOPE, NOPE, n_heads)
    prm["segdq"], prm["expdq"] = _seg_mats(n_dheads * 2 * HALF, HALF, 2 * n_dheads)
    prm["segdk"], prm["expdk"] = _seg_mats(n_kvheads * 2 * HALF, HALF, 2 * n_kvheads)

    c_f, kr_f, kd_f, vd_f, qm, km, ca, qd, kdb, va = _project(h_flat, ct, st, prm, dims)

    si = np.arange(PAGE)[:, None]
    ti_ = np.arange(PAGE)[None, :]
    bk0 = np.where(si >= ti_, _bucket_np(si - ti_), -1)
    bk1 = _bucket_np(PAGE + si - ti_)
    drow = np.arange(2 * s_len * per_kv)
    d_s = (drow // per_kv) % s_len
    d_r = drow % per_kv
    n_drow = drow.shape[0]
    assert n_drow <= PAGE
    bk_last = np.zeros((PAGE, PAGE), np.int32)
    bk_last[:n_drow] = _bucket_np(PAGE + d_s[:, None] - ti_)
    bk_new = np.full((PAGE, PAGE), -1, np.int32)
    bk_new[:n_drow] = np.where((ti_ <= d_s[:, None]) & (ti_ < s_len), _bucket_np(d_s[:, None] - ti_), -1)
    bks = [bk0] * n_dheads + [bk1] * n_dheads + [bk_last] * n_kvheads + [bk_new] * n_kvheads
    rb_t = rel_bias.astype(F32).T
    head_rows = [jnp.broadcast_to(rb_t[h][None], (PAGE, N_BUCKETS)) for h in range(n_dheads)]
    dec_rows = []
    for g in range(n_kvheads):
        sel = np.zeros((PAGE,), np.int32)
        sel[:n_drow] = g * per_kv + d_r
        dec_rows.append(rb_t[jnp.asarray(sel)])
    rts = head_rows + head_rows + dec_rows + dec_rows
    tiles, lam = _bias_prep(jnp.asarray(np.stack(bks)), jnp.stack(rts), lambda_q1[layer][None],
                            lambda_k1[layer][None], lambda_q2[layer][None], lambda_k2[layer][None], lam_init)

    def stack_bias(t):
        t = t.reshape(n_kvheads, per_kv, 1, PAGE, PAGE)
        return jnp.broadcast_to(t, (n_kvheads, per_kv, 2, PAGE, PAGE)).reshape(n_kvheads, per_kv * 2 * PAGE, PAGE)

    b0 = stack_bias(tiles[:n_dheads])
    b1 = stack_bias(tiles[n_dheads:2 * n_dheads])
    tlast = tiles[2 * n_dheads:2 * n_dheads + n_kvheads, :n_drow]
    tnew = tiles[2 * n_dheads + n_kvheads:, :n_drow]

    wuv = jnp.transpose(w_uv3, (1, 0, 2)).astype(BF16)
    gsub = g_subln[layer][None]
    cat = _prompt_attn(qm, qd, km, ca, kdb, va, wuv, b0, b1, gsub, lam, batch, lp, dims, sub_scale)

    wukt3 = jnp.transpose(w_uk3, (1, 2, 0)).astype(BF16)
    perm = np.zeros((LANES, LANES), np.float32)
    for i in range(32):
        perm[i, i] = 1.0
        perm[i, 32 + i] = -1.0
        perm[32 + i, 64 + i] = 1.0
        perm[32 + i, 96 + i] = 1.0
    qabs, qf = _sample_q(qm, prm["g_kn"][:, :NOPE], wukt3, jnp.asarray(perm, BF16), n_p, n_s, n_heads)
    hs = n_heads * s_len
    to_hs = lambda a, w: a.reshape(n_seq, s_len, n_heads, w).transpose(0, 2, 1, 3).reshape(n_seq, hs, w)
    qabs_b = to_hs(qabs, kv_lora)
    qf_b = to_hs(qf, LANES)
    krsq_rows = np.zeros((8, 2 * LANES), np.float32)
    krsq_rows[:, LANES:LANES + ROPE] = 1.0
    lhs2 = jnp.concatenate([jnp.concatenate([qf_b, jnp.zeros_like(qf_b)], axis=2),
                            jnp.broadcast_to(jnp.asarray(krsq_rows, BF16)[None], (n_seq, 8, 2 * LANES))], axis=1)
    qd_s = qd[n_p:n_p + n_s].reshape(n_seq, s_len, n_kvheads, per_kv, 2, LANES)
    qdl = qd_s.transpose(0, 2, 4, 1, 3, 5).reshape(n_seq, n_kvheads, n_drow, LANES)
    seq_rows = lambda a: a[n_p:n_p + n_s].reshape(n_seq, s_len, a.shape[1])
    pages = DECODE_PAGES if n_pages % DECODE_PAGES == 0 else 1
    t_chunk = pages * PAGE
    kc, ks = _rope_tables(jnp.arange(past + PAGE, dtype=jnp.int32))
    cst_all = jnp.concatenate([kc, ks, ks, kc], axis=1).T
    cst = cst_all[:, :past].reshape(LANES, n_pages // pages, t_chunk).transpose(1, 0, 2)
    cst_new = cst_all[:, past:]
    gkr = gk[NOPE:]
    ggt = jnp.broadcast_to(jnp.concatenate([gkr, gkr])[:, None], (LANES, LANES))
    krt_new = jnp.pad(jnp.swapaxes(seq_rows(kr_f), 1, 2), ((0, 0), (0, 0), (0, PAGE - s_len)))
    j_ = np.arange(LANES)[None, :]
    s_of_row = (np.arange(hs) % s_len)[:, None]
    mnew = jnp.asarray(np.where((j_ <= s_of_row) & (j_ < s_len), 0.0, NEG_INF).astype(np.float32))
    krp = jnp.swapaxes(cache_mla_krope, 2, 3)
    dkp = cache_diff_k.reshape(depth, n_pool, PAGE * n_kvheads, 2 * HALF)
    dvp = cache_diff_v.reshape(depth, n_pool, PAGE * n_kvheads, LANES)
    olat, od = _decode_attn(page_table, cache_mla_latent, krp, dkp, dvp, qabs_b, lhs2, qdl,
                            seq_rows(c_f), krt_new, seq_rows(kd_f), seq_rows(vd_f),
                            wukt3.reshape(n_heads * NOPE, kv_lora), cst, cst_new, ggt, tlast, tnew, mnew, lam,
                            layer, dims, pages)
    olat_t = olat.reshape(n_seq, n_heads, s_len, kv_lora).transpose(0, 2, 1, 3).reshape(n_s, n_heads * kv_lora)
    od_t = od.reshape(n_seq, n_kvheads, s_len, per_kv, LANES).transpose(0, 2, 1, 3, 4).reshape(n_s, n_dheads * LANES)
    cat_s = _sample_post(olat_t, od_t, wuv, gsub, nf - n_p, dims, sub_scale)

    wr = jnp.pad(w_router[layer].astype(F32), ((0, 0), (0, LANES - n_experts)))
    wrh, wrl = _split_bf16(wr)
    br = jnp.pad(b_router[layer].astype(F32), (0, LANES - n_experts))[None]
    h1, hn, ti, tw = _merge(cat, cat_s, h_flat, w_o[layer].astype(BF16), g_ffn[layer][None], wrh, wrl, br,
                            n_experts)

    n_valid = batch * l_tot + n_s
    n_items = n_experts + -(-(n_valid * TOP_K) // CAP)
    item_e, item_n, item_start, tok2d, slots = _route(ti, valid, n_experts, n_items)
    eo = _experts(item_e, item_n, item_start, tok2d, hn, w_gate_up, b_gate_up, w_down, b_down, layer)
    out = _combine(slots, h1, tw, eo.reshape(n_items * CAP, d_model))

    y_prompt = out[:n_p].reshape(batch, lp, d_model)[:, N_META:l_tot]
    y_sample = out[n_p:n_p + n_s].reshape(n_seq, s_len, d_model)
    pr = lambda a: a[:n_p].reshape(batch, lp, a.shape[1])[:, :l_tot][None]
    sr = lambda a: a[n_p:n_p + n_s].reshape(n_seq, s_len, a.shape[1])[None]
    kv5 = lambda a: a.reshape(a.shape[:3] + (n_kvheads, a.shape[3] // n_kvheads))
    return (y_prompt, y_sample, pr(c_f), pr(kr_f), kv5(pr(kd_f)), kv5(pr(vd_f)),
            sr(c_f), sr(kr_f), kv5(sr(kd_f)), kv5(sr(vd_f)))
```

```python
import functools
import math

import numpy as np
import jax
import jax.numpy as jnp
from jax import lax
from jax.experimental import pallas as pl
from jax.experimental.pallas import tpu as pltpu

F32 = jnp.float32
BF16 = jnp.bfloat16
NEG_INF = float("-inf")

N_META = 16
ROPE_THETA = 10000.0
N_BUCKETS = 32
MAX_DISTANCE = 128
TOP_K = 4
SWIGLU_LIMIT = 7.0
SWIGLU_ALPHA = 1.702
EPS = 1e-6
PAGE = 128
NOPE = 128
ROPE = 64
QK = NOPE + ROPE
HALF = 64

LANES = 128
TM = 256
VMEM_LIMIT = 56 * 1024 * 1024
DECODE_PAGES = 8

CAP = 2048
RS = 256
FC = 256
TC = 256
TOK_ALIGN = 1024
TOK_WIN = (CAP + TOK_ALIGN) // LANES

NT_DIMS = (((1,), (1,)), ((), ()))


def _cparams(sem):
    return pltpu.CompilerParams(dimension_semantics=sem, vmem_limit_bytes=VMEM_LIMIT)


def _dot(a, b):
    return jnp.dot(a, b, preferred_element_type=F32)


def _dot_nt(a, b):
    return lax.dot_general(a, b, NT_DIMS, preferred_element_type=F32)


def _split_bf16(x):
    hi = x.astype(BF16)
    lo = (x - hi.astype(F32)).astype(BF16)
    return hi, lo


def _seg_rsqrt(sq, seg_ref, exp_ref, width, extra=None):
    ssq = _dot(sq.astype(BF16), seg_ref[...])
    if extra is not None:
        ssq = ssq + extra
    rs = lax.rsqrt(ssq * (1.0 / width) + EPS)
    hi, lo = _split_bf16(rs)
    return _dot(hi, exp_ref[...]) + _dot(lo, exp_ref[...])


def _rms_rows(x, width):
    return lax.rsqrt(jnp.sum(x * x, axis=-1, keepdims=True) * (1.0 / width) + EPS)


def _lane_tile(x, reps):
    return x if reps == 1 else jnp.concatenate([x] * reps, axis=1)


def _bucket_np(n):
    n = np.maximum(n, 0)
    max_exact = N_BUCKETS // 2
    nf = np.maximum(n, 1).astype(np.float32)
    large = max_exact + (np.log(nf / max_exact) / math.log(MAX_DISTANCE / max_exact)
                         * (N_BUCKETS - max_exact)).astype(np.int32)
    large = np.minimum(large, N_BUCKETS - 1)
    return np.where(n < max_exact, n, large).astype(np.int32)


def _prep_kernel(bk_ref, rt_ref, lq1_ref, lk1_ref, lq2_ref, lk2_ref, tile_ref, lam_ref, *, lam_init):
    bk = bk_ref[0]
    rt = rt_ref[0]
    far = rt[:, N_BUCKETS - 1:N_BUCKETS]
    acc = jnp.zeros((PAGE, LANES), F32)
    for b in range(N_BUCKETS):
        acc = jnp.where(bk == b, rt[:, b:b + 1] - far, acc)
    tile_ref[0] = jnp.where(bk < 0, NEG_INF, acc)
    s1 = jnp.sum(lq1_ref[...] * lk1_ref[...], axis=-1, keepdims=True)
    s2 = jnp.sum(lq2_ref[...] * lk2_ref[...], axis=-1, keepdims=True)
    lam = jnp.exp(s1) - jnp.exp(s2) + lam_init
    lam_ref[...] = jnp.broadcast_to(lam, lam_ref.shape)


def _bias_prep(bk, rt, lq1, lk1, lq2, lk2, lam_init):
    n = bk.shape[0]
    vec = pl.BlockSpec((1, HALF), lambda i: (0, 0))
    return pl.pallas_call(
        functools.partial(_prep_kernel, lam_init=lam_init),
        grid=(n,),
        in_specs=[pl.BlockSpec((1, PAGE, LANES), lambda i: (i, 0, 0)),
                  pl.BlockSpec((1, PAGE, N_BUCKETS), lambda i: (i, 0, 0)),
                  vec, vec, vec, vec],
        out_specs=[pl.BlockSpec((1, PAGE, LANES), lambda i: (i, 0, 0)),
                   pl.BlockSpec((8, LANES), lambda i: (0, 0))],
        out_shape=[jax.ShapeDtypeStruct((n, PAGE, LANES), F32),
                   jax.ShapeDtypeStruct((8, LANES), F32)],
        compiler_params=_cparams(("arbitrary",)),
        name="bias_prep",
    )(bk, rt, lq1, lk1, lq2, lk2)


def _proj_kernel(x_ref, ct_ref, st_ref, gattn_ref, win_ref, gqa_ref, wq_ref, gq_ref, gkv_ref, wuk_ref,
                 gkn_ref, gkr_ref, gdq_ref, gdk_ref,
                 segq_ref, expq_ref, segk_ref, expk_ref, segdq_ref, expdq_ref, segdk_ref, expdk_ref,
                 c_ref, kr_ref, kd_ref, vd_ref, qm_ref, km_ref, ca_ref, qd_ref, kdb_ref, va_ref,
                 *, d_model, q_lora, kv_lora, n_heads, n_dheads, n_kvheads):
    x = x_ref[...]
    ct = ct_ref[...]
    st = st_ref[...]
    xn = x * _rms_rows(x, d_model) * gattn_ref[...]
    p = _dot(xn.astype(BF16), win_ref[...])
    o_kv = q_lora
    o_kr = o_kv + kv_lora
    o_dq = o_kr + LANES
    o_dk = o_dq + n_dheads * LANES
    o_dv = o_dk + n_kvheads * LANES
    qa = p[:, :q_lora]
    kv = p[:, o_kv:o_kr]
    krd = p[:, o_kr:o_dq]
    dq = p[:, o_dq:o_dk]
    dk = p[:, o_dk:o_dv]
    dv = p[:, o_dv:o_dv + n_kvheads * LANES]

    def rope(v):
        return v * ct + pltpu.roll(v, 32, 1) * st

    qan = qa * _rms_rows(qa, q_lora) * gqa_ref[...]
    q = _dot(qan.astype(BF16), wq_ref[...])
    qg = q * _seg_rsqrt(q * q, segq_ref, expq_ref, QK) * gq_ref[...]
    pieces = []
    for h in range(n_heads):
        pieces.append(qg[:, h * 256:h * 256 + LANES])
        pieces.append(rope(qg[:, h * 256 + LANES:(h + 1) * 256]))
    qm_ref[...] = jnp.concatenate(pieces, axis=1).astype(BF16)

    c = kv * _rms_rows(kv, kv_lora) * gkv_ref[...]
    c_ref[...] = c
    cb = c.astype(BF16)
    ones = jnp.ones((x.shape[0], LANES), BF16)
    ca_ref[...] = jnp.concatenate([cb, ones], axis=1)
    kr = krd[:, :ROPE]
    kr_ref[...] = kr

    kn = _dot(cb, wuk_ref[...])
    kr_ssq = jnp.sum(kr * kr, axis=-1, keepdims=True)
    rk = _seg_rsqrt(kn * kn, segk_ref, expk_ref, QK, extra=kr_ssq)
    kng = kn * rk * gkn_ref[...]
    krr = rope(krd * gkr_ref[...])
    pieces = []
    for h in range(n_heads):
        pieces.append(kng[:, h * LANES:(h + 1) * LANES])
        pieces.append(krr * rk[:, h * LANES:(h + 1) * LANES])
    km_ref[...] = jnp.concatenate(pieces, axis=1).astype(BF16)

    qd = dq * _seg_rsqrt(dq * dq, segdq_ref, expdq_ref, HALF) * gdq_ref[...]
    lane = lax.broadcasted_iota(jnp.int32, (x.shape[0], LANES), 1)
    pieces = []
    for h in range(n_dheads):
        blk = qd[:, h * LANES:(h + 1) * LANES]
        pieces.append(jnp.where(lane < HALF, blk, 0.0))
        pieces.append(jnp.where(lane >= HALF, blk, 0.0))
    qd_ref[...] = jnp.concatenate(pieces, axis=1).astype(BF16)
    kd = dk * _seg_rsqrt(dk * dk, segdk_ref, expdk_ref, HALF) * gdk_ref[...]
    kd_ref[...] = kd
    kdb_ref[...] = kd.astype(BF16)
    vd_ref[...] = dv
    dvb = dv.astype(BF16)
    va_ref[...] = jnp.concatenate([t for g in range(n_kvheads)
                                   for t in (dvb[:, g * LANES:(g + 1) * LANES], ones)], axis=1)


def _seg_mats(width, seg, n_seg, skip=None):
    m = np.zeros((width, LANES), np.float32)
    for i in range(width):
        s = i // seg
        if s < n_seg and not (skip is not None and skip(i)):
            m[i, s] = 1.0
    e = np.zeros((LANES, width), np.float32)
    for i in range(width):
        s = i // seg
        if s < n_seg:
            e[s, i] = 1.0
    return jnp.asarray(m, BF16), jnp.asarray(e, BF16)


def _project(h_flat, ct, st, prm, dims):
    nf, d_model = h_flat.shape
    n_heads, n_dheads, n_kvheads = dims["n_heads"], dims["n_dheads"], dims["n_kvheads"]
    q_lora, kv_lora = dims["q_lora"], dims["kv_lora"]
    row = lambda w: pl.BlockSpec((TM, w), lambda i: (i, 0))
    full = lambda a: pl.BlockSpec(a.shape, lambda i: (0,) * a.ndim)
    consts = [prm["g_attn"], prm["w_in"], prm["g_qa"], prm["w_q"], prm["g_q"], prm["g_kv"], prm["w_uk"],
              prm["g_kn"], prm["g_kr"], prm["g_dq"], prm["g_dk"],
              prm["segq"], prm["expq"], prm["segk"], prm["expk"], prm["segdq"], prm["expdq"],
              prm["segdk"], prm["expdk"]]
    kvw = n_kvheads * LANES
    out_w = [(kv_lora, F32), (ROPE, F32), (kvw, F32), (kvw, F32),
             (n_heads * 256, BF16), (n_heads * 256, BF16), (kv_lora + LANES, BF16),
             (n_dheads * 256, BF16), (kvw, BF16), (2 * kvw, BF16)]
    return pl.pallas_call(
        functools.partial(_proj_kernel, d_model=d_model, q_lora=q_lora, kv_lora=kv_lora,
                          n_heads=n_heads, n_dheads=n_dheads, n_kvheads=n_kvheads),
        grid=(nf // TM,),
        in_specs=[row(d_model), row(LANES), row(LANES)] + [full(a) for a in consts],
        out_specs=[row(w) for w, _ in out_w],
        out_shape=[jax.ShapeDtypeStruct((nf, w), dt) for w, dt in out_w],
        compiler_params=_cparams(("arbitrary",)),
        name="project",
    )(h_flat, ct, st, *consts)


def _sample_q_kernel(qm_ref, gkn_ref, wukt_ref, perm_ref, qabs_ref, qf_ref, *, n_heads):
    for h in range(n_heads):
        qn = qm_ref[:, h * 256:h * 256 + LANES].astype(F32) * gkn_ref[...]
        qabs_ref[:, h * 256:(h + 1) * 256] = _dot(qn.astype(BF16), wukt_ref[h]).astype(BF16)
        qr = qm_ref[:, h * 256 + LANES:(h + 1) * 256]
        qf_ref[:, h * LANES:(h + 1) * LANES] = _dot(qr, perm_ref[...]).astype(BF16)


def _sample_q(qm, gkn, wukt, perm, row0, ns, n_heads):
    blk0 = row0 // LANES
    return pl.pallas_call(
        functools.partial(_sample_q_kernel, n_heads=n_heads),
        grid=(ns // LANES,),
        in_specs=[pl.BlockSpec((LANES, n_heads * 256), lambda i: (blk0 + i, 0)),
                  pl.BlockSpec(gkn.shape, lambda i: (0, 0)),
                  pl.BlockSpec(wukt.shape, lambda i: (0, 0, 0)),
                  pl.BlockSpec(perm.shape, lambda i: (0, 0))],
        out_specs=[pl.BlockSpec((LANES, n_heads * 256), lambda i: (i, 0)),
                   pl.BlockSpec((LANES, n_heads * LANES), lambda i: (i, 0))],
        out_shape=[jax.ShapeDtypeStruct((ns, n_heads * 256), BF16),
                   jax.ShapeDtypeStruct((ns, n_heads * LANES), BF16)],
        compiler_params=_cparams(("arbitrary",)),
        name="sample_q",
    )(qm, gkn, wukt, perm)


def _prompt_attn_kernel(qm_ref, qd_ref, km_ref, ca_ref, kd_ref, va_ref, wuv_ref, b0_ref, b1_ref,
                        gsub_ref, lam_ref, cat_ref, m1, a1, m2, a2, qs_sc,
                        *, n_heads, n_kvheads, per_kv, kv_lora, sub_scale):
    i = pl.program_id(1)
    row = lax.broadcasted_iota(jnp.int32, (PAGE, PAGE), 0)
    col = lax.broadcasted_iota(jnp.int32, (PAGE, PAGE), 1)
    causal = col <= row
    n_stack = per_kv * 2
    va_w = 2 * LANES

    m1[...] = jnp.full(m1.shape, NEG_INF, F32)
    a1[...] = jnp.zeros(a1.shape, F32)
    m2[...] = jnp.full(m2.shape, NEG_INF, F32)
    a2[...] = jnp.zeros(a2.shape, F32)
    for g in range(n_kvheads):
        for t in range(n_stack):
            c0 = (g * n_stack + t) * LANES
            qs_sc[g, t * PAGE:(t + 1) * PAGE, :] = qd_ref[:, c0:c0 + LANES]

    def soft(s, m_ref, idx):
        m_old = m_ref[idx]
        m_new = jnp.maximum(m_old, jnp.max(s, axis=-1, keepdims=True))
        m_ref[idx] = m_new
        return jnp.exp(m_old - m_new), jnp.exp(s - m_new).astype(BF16)

    def block(j, bias_ref, masked):
        r0 = pl.multiple_of(j * PAGE, PAGE)
        scores = []
        for h in range(n_heads):
            s = _dot_nt(qm_ref[:, h * 256:(h + 1) * 256], km_ref[pl.ds(r0, PAGE), h * 256:(h + 1) * 256])
            scores.append(jnp.where(causal, s, NEG_INF) if masked else s)
        for g in range(n_kvheads):
            s = _dot_nt(qs_sc[g], kd_ref[pl.ds(r0, PAGE), g * LANES:(g + 1) * LANES])
            scores.append(s if bias_ref is None else s + bias_ref[g])
        softs = [soft(scores[h], m1, h) for h in range(n_heads)]
        softs += [soft(scores[n_heads + g], m2, g) for g in range(n_kvheads)]
        ca = ca_ref[pl.ds(r0, PAGE), :]
        for h in range(n_heads):
            alpha, p = softs[h]
            a1[h] = _lane_tile(alpha, (kv_lora + LANES) // LANES) * a1[h] + _dot(p, ca)
        for g in range(n_kvheads):
            alpha, p = softs[n_heads + g]
            va = va_ref[pl.ds(r0, PAGE), g * va_w:(g + 1) * va_w]
            a2[g] = _lane_tile(alpha, va_w // LANES) * a2[g] + _dot(p, va)

    def far_body(j, c):
        block(j, None, False)
        return c

    lax.fori_loop(0, jnp.maximum(i - 1, 0), far_body, 0)

    @pl.when(i >= 1)
    def _():
        block(i - 1, b1_ref, False)

    block(i, b0_ref, True)

    for h in range(n_heads):
        acc = a1[h]
        inv = 1.0 / acc[:, kv_lora:kv_lora + LANES]
        olat = acc[:, :kv_lora] * _lane_tile(inv, kv_lora // LANES)
        cat_ref[:, h * LANES:(h + 1) * LANES] = _dot(olat.astype(BF16), wuv_ref[h]).astype(BF16)
    lam = lam_ref[0:1, 0:1]
    for g in range(n_kvheads):
        acc = a2[g]
        o = acc[:, :LANES] * (1.0 / acc[:, LANES:])
        for r in range(per_kv):
            o1 = o[(2 * r) * PAGE:(2 * r + 1) * PAGE]
            o2 = o[(2 * r + 1) * PAGE:(2 * r + 2) * PAGE]
            od = o1 - lam * o2
            odn = od * _rms_rows(od, LANES) * gsub_ref[...] * sub_scale
            c0 = (n_heads + g * per_kv + r) * LANES
            cat_ref[:, c0:c0 + LANES] = odn.astype(BF16)


def _prompt_attn(qm, qd, km, ca, kdb, va, wuv, b0, b1, gsub, lam, batch, lp, dims, sub_scale):
    n_heads, n_dheads, n_kvheads = dims["n_heads"], dims["n_dheads"], dims["n_kvheads"]
    kv_lora = dims["kv_lora"]
    per_kv = n_dheads // n_kvheads
    nqb = lp // PAGE
    qrow = lambda w: pl.BlockSpec((PAGE, w), lambda b, i: (b * nqb + i, 0))
    kvrow = lambda w: pl.BlockSpec((lp, w), lambda b, i: (b, 0))
    full = lambda a: pl.BlockSpec(a.shape, lambda b, i: (0,) * a.ndim)
    cat_w = (n_heads + n_dheads) * LANES
    n_stack = per_kv * 2 * PAGE
    return pl.pallas_call(
        functools.partial(_prompt_attn_kernel, n_heads=n_heads, n_kvheads=n_kvheads, per_kv=per_kv,
                          kv_lora=kv_lora, sub_scale=sub_scale),
        grid=(batch, nqb),
        in_specs=[qrow(n_heads * 256), qrow(n_dheads * 256), kvrow(n_heads * 256), kvrow(ca.shape[1]),
                  kvrow(n_kvheads * LANES), kvrow(va.shape[1]),
                  full(wuv), full(b0), full(b1), full(gsub), full(lam)],
        out_specs=qrow(cat_w),
        out_shape=jax.ShapeDtypeStruct((batch * lp, cat_w), BF16),
        scratch_shapes=[pltpu.VMEM((n_heads, PAGE, LANES), F32),
                        pltpu.VMEM((n_heads, PAGE, kv_lora + LANES), F32),
                        pltpu.VMEM((n_kvheads, n_stack, LANES), F32),
                        pltpu.VMEM((n_kvheads, n_stack, 2 * LANES), F32),
                        pltpu.VMEM((n_kvheads, n_stack, LANES), BF16)],
        compiler_params=_cparams(("arbitrary", "arbitrary")),
        name="prompt_attn",
    )(qm, qd, km, ca, kdb, va, wuv, b0, b1, gsub, lam)


def _decode_kernel(pt_ref, lat_hbm, krp_hbm, dk_hbm, dv_hbm,
                   qabs_ref, lhs2_ref, qdl_ref, cnew_ref, krnew_ref, kdnew_ref, vdnew_ref,
                   wukt_ref, cst_ref, cstn_ref, ggt_ref, tlast_ref, tnew_ref, mnew_ref, lam_ref,
                   olat_ref, od_ref,
                   cbuf, rbuf, kbuf, vbuf, sem, lhs1,
                   *, layer, n_seq, n_chunks, pages, n_heads, n_kvheads, s_len, kv_lora, ds_rows):
    b = pl.program_id(0)
    t_chunk = pages * PAGE
    hs = n_heads * s_len
    n_nope = n_heads * NOPE

    def page_copies(seq, chunk, slot):
        cps = []
        for p in range(pages):
            pg = pt_ref[seq, chunk * pages + p]
            cps.append(pltpu.make_async_copy(lat_hbm.at[layer, pg],
                                             cbuf.at[slot, pl.ds(p * PAGE, PAGE), :], sem.at[slot]))
            cps.append(pltpu.make_async_copy(krp_hbm.at[layer, pg],
                                             rbuf.at[slot, :, pl.ds(p * PAGE, PAGE)], sem.at[slot]))
            cps.append(pltpu.make_async_copy(dk_hbm.at[layer, pg],
                                             kbuf.at[slot, pl.ds(p * n_kvheads * PAGE, n_kvheads * PAGE), :],
                                             sem.at[slot]))
            cps.append(pltpu.make_async_copy(dv_hbm.at[layer, pg],
                                             vbuf.at[slot, pl.ds(p * n_kvheads * PAGE, n_kvheads * PAGE), :],
                                             sem.at[slot]))
        return cps

    def start_chunk(seq, chunk, slot):
        for cp in page_copies(seq, chunk, slot):
            cp.start()

    def wait_chunk(seq, chunk, slot):
        for cp in page_copies(seq, chunk, slot):
            cp.wait()

    @pl.when(b == 0)
    def _():
        start_chunk(0, 0, 0)
        lhs1[0:n_nope, :] = wukt_ref[...]

    lhs1[n_nope:n_nope + hs, :] = qabs_ref[0]

    def fresh(rows, width):
        return (jnp.full((rows, LANES), NEG_INF, F32), jnp.zeros((rows, LANES), F32),
                jnp.zeros((rows, width), F32))

    state0 = (fresh(hs, kv_lora),) + tuple(fresh(ds_rows, LANES) for _ in range(n_kvheads))

    lhs2 = lhs2_ref[0]
    ggt = ggt_ref[...]

    def lane_fold(p):
        acc = p[:, 0:LANES]
        for k in range(1, p.shape[1] // LANES):
            acc = acc + p[:, k * LANES:(k + 1) * LANES]
        return acc

    def key_products(c32, krt, cst):
        reps = c32.shape[0] // LANES
        kall = _dot_nt(lhs1[...], c32.astype(BF16))
        xxt = jnp.concatenate([krt, krt], axis=0)
        f2t = jnp.concatenate([xxt * _lane_tile(ggt, reps) * cst, xxt * xxt], axis=0).astype(BF16)
        return kall, _dot(lhs2, f2t)

    def key_norms(kall, s2):
        krsq = s2[hs:hs + 8]
        rows = []
        for h in range(n_heads):
            kt = kall[h * NOPE:(h + 1) * NOPE]
            nsq = jnp.sum(kt * kt, axis=0, keepdims=True)
            rinv = lax.rsqrt((nsq + krsq) * (1.0 / QK) + EPS)
            r0 = n_nope + h * s_len
            rows.append((kall[r0:r0 + s_len] + s2[h * s_len:(h + 1) * s_len]) * rinv)
        return jnp.concatenate(rows, axis=0)

    def diff_scores(kd_g, diff_bias):
        out = []
        for g in range(n_kvheads):
            sg = _dot_nt(qdl_ref[0, g], kd_g[g].astype(BF16))
            out.append(sg if diff_bias is None else sg + diff_bias[g])
        return out

    def soft(scores, state):
        out = []
        for s, (m_old, l_old, _) in zip(scores, state):
            m_new = jnp.maximum(m_old, jnp.max(s, axis=-1, keepdims=True))
            alpha = jnp.exp(m_old - m_new)
            p = jnp.exp(s - _lane_tile(m_new, s.shape[1] // LANES))
            out.append((m_new, alpha * l_old + lane_fold(p), alpha, p.astype(BF16)))
        return out

    def value_products(softs, state, values):
        new = []
        for (m_new, l_new, alpha, p), (_, _, a_old), v in zip(softs, state, values):
            new.append((m_new, l_new, _lane_tile(alpha, a_old.shape[1] // LANES) * a_old + _dot(p, v)))
        return tuple(new)

    def slot_of(chunk):
        return lax.rem(b * n_chunks + chunk, 3)

    def fetch(chunk):
        wait_chunk(b, chunk, slot_of(chunk))
        last = chunk + 1 == n_chunks
        nseq = jnp.where(last, jnp.minimum(b + 1, n_seq - 1), b)
        start_chunk(nseq, jnp.where(last, 0, chunk + 1), slot_of(chunk + 1))

    def chunk_keys(chunk):
        slot = slot_of(chunk)
        return key_products(cbuf[slot], rbuf[slot], cst_ref[chunk])

    def strided(buf, slot):
        return [buf[slot, pl.ds(g, t_chunk, stride=n_kvheads), :] for g in range(n_kvheads)]

    def chunk_values(chunk):
        slot = slot_of(chunk)
        return [cbuf[slot].astype(BF16)] + [v.astype(BF16) for v in strided(vbuf, slot)]

    def body(chunk, carry):
        s_prev, state = carry
        fetch(chunk)
        sd = diff_scores(strided(kbuf, slot_of(chunk - 1)), None)
        kall, s2 = chunk_keys(chunk)
        softs = soft([s_prev] + sd, state)
        state = value_products(softs, state, chunk_values(chunk - 1))
        return key_norms(kall, s2), state

    fetch(0)
    s_last, state = lax.fori_loop(1, n_chunks, body, (key_norms(*chunk_keys(0)), state0))

    zeros = jnp.zeros((ds_rows, t_chunk - PAGE), F32)
    bias = [jnp.concatenate([zeros, tlast_ref[g]], axis=1) if t_chunk > PAGE else tlast_ref[g]
            for g in range(n_kvheads)]
    sd = diff_scores(strided(kbuf, slot_of(n_chunks - 1)), bias)
    pad = PAGE - s_len
    padr = lambda x: jnp.concatenate([x, jnp.zeros((pad, x.shape[1]), x.dtype)], axis=0)
    c_new = padr(cnew_ref[0])
    kd_new = padr(kdnew_ref[0])
    vd_new = padr(vdnew_ref[0])
    kall, s2 = key_products(c_new, krnew_ref[0], cstn_ref[...])
    state = value_products(soft([s_last] + sd, state), state, chunk_values(n_chunks - 1))
    sd = diff_scores([kd_new[:, g * LANES:(g + 1) * LANES] for g in range(n_kvheads)],
                     [tnew_ref[g] for g in range(n_kvheads)])
    s_new = key_norms(kall, s2) + mnew_ref[...]
    values = [c_new.astype(BF16)] + [vd_new[:, g * LANES:(g + 1) * LANES].astype(BF16) for g in range(n_kvheads)]
    state = value_products(soft([s_new] + sd, state), state, values)

    @pl.when(b == n_seq - 1)
    def _():
        wait_chunk(b, 0, slot_of(n_chunks))

    _, l1, a1 = state[0]
    olat_ref[0] = a1 * (1.0 / jnp.sum(l1, axis=-1, keepdims=True))
    lam = lam_ref[0:1, 0:1]
    half = ds_rows // 2
    for g in range(n_kvheads):
        _, l2, a2 = state[1 + g]
        o = a2 * (1.0 / jnp.sum(l2, axis=-1, keepdims=True))
        od_ref[0, g] = o[:half] - lam * o[half:]


def _decode_attn(page_table, lat, krp, dkp, dvp, qabs, lhs2, qdl, c_new, krt_new, kd_new, vd_new,
                 wukt, cst, cst_new, ggt, tlast, tnew, mnew, lam, layer, dims, pages):
    n_seq, n_pages = page_table.shape
    n_heads, n_kvheads = dims["n_heads"], dims["n_kvheads"]
    s_len = c_new.shape[1]
    kv_lora = c_new.shape[2]
    n_chunks = n_pages // pages
    t_chunk = pages * PAGE
    hs = n_heads * s_len
    ds_rows = qdl.shape[2]
    seq3 = lambda a: pl.BlockSpec((1,) + a.shape[1:], lambda b, pt: (b,) + (0,) * (a.ndim - 1))
    full = lambda a: pl.BlockSpec(a.shape, lambda b, pt: (0,) * a.ndim)
    hbm = pl.BlockSpec(memory_space=pl.ANY)
    grid_spec = pltpu.PrefetchScalarGridSpec(
        num_scalar_prefetch=1,
        grid=(n_seq,),
        in_specs=[hbm, hbm, hbm, hbm,
                  seq3(qabs), seq3(lhs2), seq3(qdl), seq3(c_new), seq3(krt_new), seq3(kd_new), seq3(vd_new),
                  full(wukt), full(cst), full(cst_new), full(ggt), full(tlast), full(tnew), full(mnew), full(lam)],
        out_specs=[pl.BlockSpec((1, hs, kv_lora), lambda b, pt: (b, 0, 0)),
                   pl.BlockSpec((1, n_kvheads, ds_rows // 2, LANES), lambda b, pt: (b, 0, 0, 0))],
        scratch_shapes=[pltpu.VMEM((3, t_chunk, kv_lora), F32),
                        pltpu.VMEM((3, ROPE, t_chunk), F32),
                        pltpu.VMEM((3, n_kvheads * t_chunk, LANES), F32),
                        pltpu.VMEM((3, n_kvheads * t_chunk, LANES), F32),
                        pltpu.SemaphoreType.DMA((3,)),
                        pltpu.VMEM((n_heads * NOPE + hs, kv_lora), BF16)],
    )
    return pl.pallas_call(
        functools.partial(_decode_kernel, layer=layer, n_seq=n_seq, n_chunks=n_chunks, pages=pages,
                          n_heads=n_heads, n_kvheads=n_kvheads, s_len=s_len, kv_lora=kv_lora,
                          ds_rows=ds_rows),
        grid_spec=grid_spec,
        out_shape=[jax.ShapeDtypeStruct((n_seq, hs, kv_lora), F32),
                   jax.ShapeDtypeStruct((n_seq, n_kvheads, ds_rows // 2, LANES), F32)],
        compiler_params=_cparams(("arbitrary",)),
        name="decode_attn",
    )(page_table, lat, krp, dkp, dvp, qabs, lhs2, qdl, c_new, krt_new, kd_new, vd_new,
      wukt, cst, cst_new, ggt, tlast, tnew, mnew, lam)


def _sample_post_kernel(olat_ref, od_ref, wuv_ref, gsub_ref, cat_ref, *, n_heads, n_dheads, kv_lora, sub_scale):
    for h in range(n_heads):
        o = olat_ref[:, h * kv_lora:(h + 1) * kv_lora].astype(BF16)
        cat_ref[:, h * LANES:(h + 1) * LANES] = _dot(o, wuv_ref[h]).astype(BF16)
    for h in range(n_dheads):
        od = od_ref[:, h * LANES:(h + 1) * LANES]
        odn = od * _rms_rows(od, LANES) * gsub_ref[...] * sub_scale
        cat_ref[:, (n_heads + h) * LANES:(n_heads + h + 1) * LANES] = odn.astype(BF16)


def _sample_post(olat, od, wuv, gsub, n_rows, dims, sub_scale):
    ns = olat.shape[0]
    n_heads, n_dheads = dims["n_heads"], dims["n_dheads"]
    kv_lora = dims["kv_lora"]
    cat_w = (n_heads + n_dheads) * LANES
    return pl.pallas_call(
        functools.partial(_sample_post_kernel, n_heads=n_heads, n_dheads=n_dheads, kv_lora=kv_lora,
                          sub_scale=sub_scale),
        grid=(ns // LANES,),
        in_specs=[pl.BlockSpec((LANES, olat.shape[1]), lambda i: (i, 0)),
                  pl.BlockSpec((LANES, od.shape[1]), lambda i: (i, 0)),
                  pl.BlockSpec(wuv.shape, lambda i: (0, 0, 0)),
                  pl.BlockSpec(gsub.shape, lambda i: (0, 0))],
        out_specs=pl.BlockSpec((LANES, cat_w), lambda i: (i, 0)),
        out_shape=jax.ShapeDtypeStruct((n_rows, cat_w), BF16),
        compiler_params=_cparams(("arbitrary",)),
        name="sample_post",
    )(olat, od, wuv, gsub)


def _merge_kernel(catp_ref, cats_ref, h_ref, wo_ref, gffn_ref, wrh_ref, wrl_ref, br_ref,
                  h1_ref, hn_ref, ti_ref, tw_ref, *, d_model, n_experts, n_pblocks):
    cat = jnp.where(pl.program_id(0) < n_pblocks, catp_ref[...], cats_ref[...])
    h1 = h_ref[...] + _dot(cat, wo_ref[...])
    h1_ref[...] = h1
    hn = h1 * _rms_rows(h1, d_model) * gffn_ref[...]
    hn_ref[...] = hn
    hi, lo = _split_bf16(hn)
    logits = _dot(hi, wrh_ref[...]) + _dot(lo, wrh_ref[...]) + _dot(hi, wrl_ref[...]) + br_ref[...]
    lane = lax.broadcasted_iota(jnp.int32, logits.shape, 1)
    lane_f = lane.astype(F32)
    x = jnp.where(lane < n_experts, logits, NEG_INF)
    vals, idxs = [], []
    for _ in range(TOP_K):
        mk = jnp.max(x, axis=-1, keepdims=True)
        ik = jnp.min(jnp.where(x == mk, lane_f, float(LANES)), axis=-1, keepdims=True).astype(jnp.int32)
        vals.append(mk)
        idxs.append(ik)
        x = jnp.where(lane == ik, NEG_INF, x)
    es = [jnp.exp(v - vals[0]) for v in vals]
    den = es[0]
    for e in es[1:]:
        den = den + e
    inv = 1.0 / den
    ti = jnp.zeros(logits.shape, jnp.int32)
    tw = jnp.zeros(logits.shape, F32)
    for k in range(TOP_K):
        ti = jnp.where(lane == k, idxs[k], ti)
        tw = jnp.where(lane == k, es[k] * inv, tw)
    ti_ref[...] = ti
    tw_ref[...] = tw


def _merge(cat_p, cat_s, h_flat, wo, gffn, wrh, wrl, br, n_experts):
    nf, d_model = h_flat.shape
    n_pblocks = cat_p.shape[0] // TM
    n_sblocks = cat_s.shape[0] // TM
    row = lambda w: pl.BlockSpec((TM, w), lambda i: (i, 0))
    full = lambda a: pl.BlockSpec(a.shape, lambda i: (0,) * a.ndim)
    return pl.pallas_call(
        functools.partial(_merge_kernel, d_model=d_model, n_experts=n_experts, n_pblocks=n_pblocks),
        grid=(nf // TM,),
        in_specs=[pl.BlockSpec((TM, cat_p.shape[1]), lambda i: (jnp.minimum(i, n_pblocks - 1), 0)),
                  pl.BlockSpec((TM, cat_s.shape[1]),
                               lambda i: (jnp.clip(i - n_pblocks, 0, n_sblocks - 1), 0)),
                  row(d_model), full(wo), full(gffn), full(wrh), full(wrl), full(br)],
        out_specs=[row(d_model), row(d_model), row(LANES), row(LANES)],
        out_shape=[jax.ShapeDtypeStruct((nf, d_model), F32), jax.ShapeDtypeStruct((nf, d_model), F32),
                   jax.ShapeDtypeStruct((nf, LANES), jnp.int32), jax.ShapeDtypeStruct((nf, LANES), F32)],
        compiler_params=_cparams(("arbitrary",)),
        name="merge",
    )(cat_p, cat_s, h_flat, wo, gffn, wrh, wrl, br)


def _expert_kernel(ie_ref, in_ref, is_ref, tok_hbm, hn_hbm, wg_ref, wu_ref, bg_ref, bu_ref, wd_ref, bd_ref,
                   out_hbm, tok_smem, pend, xs, acc, gsem, tsem, osem, *, n_fc, n_items):
    w = pl.program_id(0)
    f = pl.program_id(1)
    n = in_ref[w]
    n_sub = (n + RS - 1) // RS

    def out_copy(item, t):
        r0 = pl.multiple_of(t * RS, RS)
        return pltpu.make_async_copy(acc.at[pl.ds(r0, RS), :], out_hbm.at[item, pl.ds(r0, RS), :], osem)

    def wait_pending():
        def wbody(t, c):
            out_copy(0, t).wait()
            return c

        lax.fori_loop(0, pend[0], wbody, 0)
        pend[0] = 0

    @pl.when(jnp.logical_and(w == 0, f == 0))
    def _():
        pend[0] = 0

    @pl.when(jnp.logical_and(f == 0, n > 0))
    def _():
        start = is_ref[w]
        win0 = pl.multiple_of((start // TOK_ALIGN) * (TOK_ALIGN // LANES), TOK_ALIGN // LANES)
        off = start % TOK_ALIGN
        cp = pltpu.make_async_copy(tok_hbm.at[pl.ds(win0, TOK_WIN), :], tok_smem, tsem)
        cp.start()
        cp.wait()

        def issue(r8, c):
            for u in range(8):
                r = r8 * 8 + u
                pos = off + r
                tok = tok_smem[lax.shift_right_logical(pos, 7), lax.bitwise_and(pos, LANES - 1)]
                pltpu.make_async_copy(hn_hbm.at[pl.ds(tok, 1), :], xs.at[pl.ds(r, 1), :], gsem).start()
            return c

        lax.fori_loop(0, n_sub * (RS // 8), issue, 0)
        wait_pending()

        def prep(t, c):
            r0 = pl.multiple_of(t * RS, RS)
            acc[pl.ds(r0, RS), :] = jnp.zeros((RS, acc.shape[1]), F32)
            pltpu.make_async_copy(hn_hbm.at[pl.ds(0, RS), :], xs.at[pl.ds(r0, RS), :], gsem).wait()
            return c

        lax.fori_loop(0, n_sub, prep, 0)

    @pl.when(n > 0)
    def _():
        wg = wg_ref[0, 0].astype(BF16)
        wu = wu_ref[0, 0].astype(BF16)
        wd = wd_ref[0, 0].astype(BF16)
        bg = bg_ref[0, 0]
        bu = bu_ref[0, 0]

        def sub(t, c):
            r0 = pl.multiple_of(t * RS, RS)
            x = xs[pl.ds(r0, RS), :].astype(BF16)
            gate = jnp.minimum(_dot(x, wg) + bg, SWIGLU_LIMIT)
            up = jnp.clip(_dot(x, wu) + bu, -SWIGLU_LIMIT, SWIGLU_LIMIT)
            act = (up + 1.0) * gate * jax.nn.sigmoid(SWIGLU_ALPHA * gate)
            acc[pl.ds(r0, RS), :] += _dot(act.astype(BF16), wd)
            return c

        lax.fori_loop(0, n_sub, sub, 0)

    @pl.when(jnp.logical_and(f == n_fc - 1, n > 0))
    def _():
        def emit(t, c):
            r0 = pl.multiple_of(t * RS, RS)
            acc[pl.ds(r0, RS), :] = acc[pl.ds(r0, RS), :] + bd_ref[...]
            out_copy(w, t).start()
            return c

        lax.fori_loop(0, n_sub, emit, 0)
        pend[0] = n_sub

    @pl.when(jnp.logical_and(w == n_items - 1, f == n_fc - 1))
    def _():
        wait_pending()


def _experts(item_e, item_n, item_start, tok2d, hn, w_gate_up, b_gate_up, w_down, b_down, layer):
    n_items = item_e.shape[0]
    d_model = hn.shape[1]
    d_ff = w_down.shape[2]
    n_fc = d_ff // FC
    bgu = b_gate_up.reshape(b_gate_up.shape[0], b_gate_up.shape[1], 1, 2 * d_ff)
    bd = b_down.reshape(b_down.shape[0], b_down.shape[1], 1, d_model)
    grid_spec = pltpu.PrefetchScalarGridSpec(
        num_scalar_prefetch=3,
        grid=(n_items, n_fc),
        in_specs=[pl.BlockSpec(memory_space=pl.ANY),
                  pl.BlockSpec(memory_space=pl.ANY),
                  pl.BlockSpec((1, 1, d_model, FC), lambda w, f, ie, inn, ist: (layer, ie[w], 0, f)),
                  pl.BlockSpec((1, 1, d_model, FC), lambda w, f, ie, inn, ist: (layer, ie[w], 0, n_fc + f)),
                  pl.BlockSpec((1, 1, 1, FC), lambda w, f, ie, inn, ist: (layer, ie[w], 0, f)),
                  pl.BlockSpec((1, 1, 1, FC), lambda w, f, ie, inn, ist: (layer, ie[w], 0, n_fc + f)),
                  pl.BlockSpec((1, 1, FC, d_model), lambda w, f, ie, inn, ist: (layer, ie[w], f, 0)),
                  pl.BlockSpec((None, None, 1, d_model), lambda w, f, ie, inn, ist: (layer, ie[w], 0, 0))],
        out_specs=pl.BlockSpec(memory_space=pl.ANY),
        scratch_shapes=[pltpu.SMEM((TOK_WIN, LANES), jnp.int32),
                        pltpu.SMEM((1,), jnp.int32),
                        pltpu.VMEM((CAP, d_model), F32),
                        pltpu.VMEM((CAP, d_model), F32),
                        pltpu.SemaphoreType.DMA(()), pltpu.SemaphoreType.DMA(()), pltpu.SemaphoreType.DMA(())],
    )
    return pl.pallas_call(
        functools.partial(_expert_kernel, n_fc=n_fc, n_items=n_items),
        grid_spec=grid_spec,
        out_shape=jax.ShapeDtypeStruct((n_items, CAP, d_model), F32),
        compiler_params=_cparams(("arbitrary", "arbitrary")),
        name="experts",
    )(item_e, item_n, item_start, tok2d, hn, w_gate_up, w_gate_up, bgu, bgu, w_down, bd)


def _combine_kernel(slot_ref, h1_ref, tw_ref, eo_hbm, out_ref, slot_smem, buf, ssem, gsem):
    cp = pltpu.make_async_copy(slot_ref.at[0, 0], slot_smem, ssem)
    cp.start()
    cp.wait()

    def issue(r, c):
        for k in range(TOP_K):
            pltpu.make_async_copy(eo_hbm.at[pl.ds(slot_smem[r * TOP_K + k], 1), :],
                                  buf.at[k, pl.ds(r, 1), :], gsem).start()
        return c

    lax.fori_loop(0, TC, issue, 0, unroll=2)
    for k in range(TOP_K):
        pltpu.make_async_copy(eo_hbm.at[pl.ds(0, TC), :], buf.at[k], gsem).wait()
    tw = tw_ref[...]
    out = h1_ref[...]
    for k in range(TOP_K):
        out = out + tw[:, k:k + 1] * buf[k]
    out_ref[...] = out


def _combine(slots, h1, tw, eo):
    nf, d_model = h1.shape
    row = lambda w: pl.BlockSpec((TC, w), lambda i: (i, 0))
    return pl.pallas_call(
        _combine_kernel,
        grid=(nf // TC,),
        in_specs=[pl.BlockSpec((1, 1, TC * TOP_K), lambda i: (i, 0, 0)),
                  row(d_model), row(LANES), pl.BlockSpec(memory_space=pl.ANY)],
        out_specs=row(d_model),
        out_shape=jax.ShapeDtypeStruct((nf, d_model), F32),
        scratch_shapes=[pltpu.SMEM((TC * TOP_K,), jnp.int32),
                        pltpu.VMEM((TOP_K, TC, d_model), F32),
                        pltpu.SemaphoreType.DMA(()), pltpu.SemaphoreType.DMA(())],
        compiler_params=_cparams(("arbitrary",)),
        name="combine",
    )(slots, h1, tw, eo)


def _route(ti, valid, n_experts, n_items):
    nf = ti.shape[0]
    n_flat = nf * TOP_K
    e_flat = jnp.where(valid[:, None], ti[:, :TOP_K], n_experts).reshape(-1)
    onehot = (e_flat[:, None] == jnp.arange(n_experts, dtype=jnp.int32)[None, :]).astype(jnp.int32)
    csum = jnp.cumsum(onehot, axis=0)
    counts = csum[-1]
    rank = jnp.sum(onehot * csum, axis=1) - 1
    seg_start = jnp.cumsum(counts) - counts
    items_per = (counts + CAP - 1) // CAP
    item_cum = jnp.cumsum(items_per)
    item_first = item_cum - items_per
    total = item_cum[-1]
    w = jnp.arange(n_items, dtype=jnp.int32)
    e_of_w = jnp.minimum(jnp.searchsorted(item_cum, w, side="right"), n_experts - 1).astype(jnp.int32)
    j = w - item_first[e_of_w]
    active = w < total
    e_last = e_of_w[jnp.maximum(total - 1, 0)]
    item_e = jnp.where(active, e_of_w, e_last).astype(jnp.int32)
    item_n = jnp.where(active, jnp.clip(counts[e_of_w] - j * CAP, 0, CAP), 0).astype(jnp.int32)
    item_start = jnp.where(active, seg_start[e_of_w] + j * CAP, 0).astype(jnp.int32)
    order = jnp.argsort(e_flat, stable=True).astype(jnp.int32)
    n_tok = -(-(n_flat + CAP + TOK_ALIGN) // TOK_ALIGN) * TOK_ALIGN
    tok2d = jnp.pad(order // TOP_K, (0, n_tok - n_flat)).reshape(n_tok // LANES, LANES)
    first_of = jnp.sum(onehot * item_first[None, :], axis=1)
    slot_flat = jnp.where(e_flat < n_experts, (first_of + rank // CAP) * CAP + rank % CAP, 0).astype(jnp.int32)
    return item_e, item_n, item_start, tok2d, slot_flat.reshape(nf // TC, 1, TC * TOP_K)


def _rope_tables(pos):
    half = ROPE // 2
    inv = jnp.power(ROPE_THETA, -jnp.arange(half, dtype=F32) / half)
    ang = pos.astype(F32)[:, None] * inv[None, :]
    return jnp.cos(ang), jnp.sin(ang)


def kernel(x_prompt, x_sample, cache_mla_latent, cache_mla_krope, cache_diff_k, cache_diff_v, page_table,
           meta_tokens, rel_bias, g_attn, w_in, g_qa, w_qb, g_kv, w_kvb, g_mla_q, g_mla_k, g_diff_q, g_diff_k,
           lambda_q1, lambda_k1, lambda_q2, lambda_k2, g_subln, w_o, g_ffn, w_router, b_router,
           w_gate_up, b_gate_up, w_down, b_down):
    batch, seq, d_model = x_prompt.shape
    n_seq, s_len = x_sample.shape[0], x_sample.shape[1]
    depth = g_attn.shape[0]
    q_lora = g_qa.shape[1]
    kv_lora = g_kv.shape[1]
    n_heads = w_qb.shape[2]
    n_dheads = rel_bias.shape[1]
    n_kvheads = cache_diff_k.shape[3]
    per_kv = n_dheads // n_kvheads
    n_experts = w_router.shape[2]
    n_pool = cache_mla_latent.shape[1]
    n_pages = page_table.shape[1]
    past = n_pages * PAGE
    assert depth == 1 and w_qb.shape[3] == QK and cache_mla_krope.shape[3] == ROPE
    assert cache_diff_k.shape[4] == 2 * HALF and cache_diff_v.shape[4] == LANES and w_kvb.shape[3] == 2 * NOPE
    assert s_len == 8 and n_heads == 8 and n_experts <= LANES
    dims = dict(n_heads=n_heads, n_dheads=n_dheads, n_kvheads=n_kvheads, q_lora=q_lora, kv_lora=kv_lora)
    layer = 0
    lam_init = 0.8 - 0.6 * math.exp(-0.3 * layer)
    sub_scale = 1.0 - lam_init
    mla_scale = QK ** -0.5
    diff_scale = HALF ** -0.5

    l_tot = seq + N_META
    lp = -(-l_tot // PAGE) * PAGE
    n_p = batch * lp
    n_s = n_seq * s_len
    assert n_s % LANES == 0 and n_p % TM == 0
    nf = -(-(n_p + n_s) // TM) * TM
    meta = jnp.broadcast_to(meta_tokens.astype(F32)[None], (batch, N_META, d_model))
    h_p = jnp.concatenate([meta, x_prompt, jnp.zeros((batch, lp - l_tot, d_model), F32)], axis=1)
    h_flat = jnp.concatenate([h_p.reshape(n_p, d_model), x_sample.reshape(n_s, d_model),
                              jnp.zeros((nf - n_p - n_s, d_model), F32)], axis=0)
    pos = jnp.concatenate([jnp.tile(jnp.arange(lp, dtype=jnp.int32), batch),
                           jnp.tile(past + jnp.arange(s_len, dtype=jnp.int32), n_seq),
                           jnp.zeros((nf - n_p - n_s,), jnp.int32)])
    cos, sin = _rope_tables(pos)
    z = jnp.zeros_like(cos)
    ct = jnp.concatenate([cos, cos, z, z], axis=1)
    st = jnp.concatenate([-sin, sin, z, z], axis=1)
    row_idx = jnp.arange(nf, dtype=jnp.int32)
    valid = jnp.where(row_idx < n_p, (row_idx % lp) < l_tot, row_idx < n_p + n_s)

    w_in_l = w_in[layer]
    o = np.cumsum([0, q_lora, kv_lora, ROPE, n_dheads * 2 * HALF, n_kvheads * 2 * HALF, n_kvheads * LANES])
    w_in_p = jnp.concatenate([w_in_l[:, o[0]:o[2]], w_in_l[:, o[2]:o[3]], w_in_l[:, o[2]:o[3]],
                              w_in_l[:, o[3]:o[6]]], axis=1).astype(BF16)
    wq = w_qb[layer]
    w_q = jnp.concatenate([wq, wq[:, :, NOPE:]], axis=2).reshape(q_lora, n_heads * 256).astype(BF16)
    gq = g_mla_q[layer] * mla_scale
    g_q = jnp.tile(jnp.concatenate([gq, gq[NOPE:]]), n_heads)[None]
    gk = g_mla_k[layer]
    w_uk3 = w_kvb[layer][:, :, :NOPE]
    w_uv3 = w_kvb[layer][:, :, NOPE:]
    prm = dict(
        g_attn=g_attn[layer][None], w_in=w_in_p, g_qa=g_qa[layer][None], w_q=w_q, g_q=g_q,
        g_kv=g_kv[layer][None], w_uk=w_uk3.reshape(kv_lora, n_heads * NOPE).astype(BF16),
        g_kn=jnp.tile(gk[:NOPE], n_heads)[None], g_kr=jnp.concatenate([gk[NOPE:], gk[NOPE:]])[None],
        g_dq=jnp.tile(g_diff_q[layer] * diff_scale, 2 * n_dheads)[None],
        g_dk=jnp.tile(g_diff_k[layer], 2 * n_kvheads)[None])
    prm["segq"], prm["expq"] = _seg_mats(n_heads * 256, 256, n_heads, skip=lambda i: i % 256 >= QK)
    prm["segk"], prm["expk"] = _seg_mats(n_heads * NOPE, NOPE, n_heads)
    prm["segdq"], prm["expdq"] = _seg_mats(n_dheads * 2 * HALF, HALF, 2 * n_dheads)
    prm["segdk"], prm["expdk"] = _seg_mats(n_kvheads * 2 * HALF, HALF, 2 * n_kvheads)

    c_f, kr_f, kd_f, vd_f, qm, km, ca, qd, kdb, va = _project(h_flat, ct, st, prm, dims)

    si = np.arange(PAGE)[:, None]
    ti_ = np.arange(PAGE)[None, :]
    bk0 = np.where(si >= ti_, _bucket_np(si - ti_), -1)
    bk1 = _bucket_np(PAGE + si - ti_)
    drow = np.arange(2 * s_len * per_kv)
    d_s = (drow // per_kv) % s_len
    d_r = drow % per_kv
    n_drow = drow.shape[0]
    assert n_drow <= PAGE
    bk_last = np.zeros((PAGE, PAGE), np.int32)
    bk_last[:n_drow] = _bucket_np(PAGE + d_s[:, None] - ti_)
    bk_new = np.full((PAGE, PAGE), -1, np.int32)
    bk_new[:n_drow] = np.where((ti_ <= d_s[:, None]) & (ti_ < s_len), _bucket_np(d_s[:, None] - ti_), -1)
    bks = [bk0] * n_dheads + [bk1] * n_dheads + [bk_last] * n_kvheads + [bk_new] * n_kvheads
    rb_t = rel_bias.astype(F32).T
    head_rows = [jnp.broadcast_to(rb_t[h][None], (PAGE, N_BUCKETS)) for h in range(n_dheads)]
    dec_rows = []
    for g in range(n_kvheads):
        sel = np.zeros((PAGE,), np.int32)
        sel[:n_drow] = g * per_kv + d_r
        dec_rows.append(rb_t[jnp.asarray(sel)])
    rts = head_rows + head_rows + dec_rows + dec_rows
    tiles, lam = _bias_prep(jnp.asarray(np.stack(bks)), jnp.stack(rts), lambda_q1[layer][None],
                            lambda_k1[layer][None], lambda_q2[layer][None], lambda_k2[layer][None], lam_init)

    def stack_bias(t):
        t = t.reshape(n_kvheads, per_kv, 1, PAGE, PAGE)
        return jnp.broadcast_to(t, (n_kvheads, per_kv, 2, PAGE, PAGE)).reshape(n_kvheads, per_kv * 2 * PAGE, PAGE)

    b0 = stack_bias(tiles[:n_dheads])
    b1 = stack_bias(tiles[n_dheads:2 * n_dheads])
    tlast = tiles[2 * n_dheads:2 * n_dheads + n_kvheads, :n_drow]
    tnew = tiles[2 * n_dheads + n_kvheads:, :n_drow]

    wuv = jnp.transpose(w_uv3, (1, 0, 2)).astype(BF16)
    gsub = g_subln[layer][None]
    cat = _prompt_attn(qm, qd, km, ca, kdb, va, wuv, b0, b1, gsub, lam, batch, lp, dims, sub_scale)

    wukt3 = jnp.transpose(w_uk3, (1, 2, 0)).astype(BF16)
    perm = np.zeros((LANES, LANES), np.float32)
    for i in range(32):
        perm[i, i] = 1.0
        perm[i, 32 + i] = -1.0
        perm[32 + i, 64 + i] = 1.0
        perm[32 + i, 96 + i] = 1.0
    qabs, qf = _sample_q(qm, prm["g_kn"][:, :NOPE], wukt3, jnp.asarray(perm, BF16), n_p, n_s, n_heads)
    hs = n_heads * s_len
    to_hs = lambda a, w: a.reshape(n_seq, s_len, n_heads, w).transpose(0, 2, 1, 3).reshape(n_seq, hs, w)
    qabs_b = to_hs(qabs, kv_lora)
    qf_b = to_hs(qf, LANES)
    krsq_rows = np.zeros((8, 2 * LANES), np.float32)
    krsq_rows[:, LANES:LANES + ROPE] = 1.0
    lhs2 = jnp.concatenate([jnp.concatenate([qf_b, jnp.zeros_like(qf_b)], axis=2),
                            jnp.broadcast_to(jnp.asarray(krsq_rows, BF16)[None], (n_seq, 8, 2 * LANES))], axis=1)
    qd_s = qd[n_p:n_p + n_s].reshape(n_seq, s_len, n_kvheads, per_kv, 2, LANES)
    qdl = qd_s.transpose(0, 2, 4, 1, 3, 5).reshape(n_seq, n_kvheads, n_drow, LANES)
    seq_rows = lambda a: a[n_p:n_p + n_s].reshape(n_seq, s_len, a.shape[1])
    pages = DECODE_PAGES if n_pages % DECODE_PAGES == 0 else 1
    t_chunk = pages * PAGE
    kc, ks = _rope_tables(jnp.arange(past + PAGE, dtype=jnp.int32))
    cst_all = jnp.concatenate([kc, ks, ks, kc], axis=1).T
    cst = cst_all[:, :past].reshape(LANES, n_pages // pages, t_chunk).transpose(1, 0, 2)
    cst_new = cst_all[:, past:]
    gkr = gk[NOPE:]
    ggt = jnp.broadcast_to(jnp.concatenate([gkr, gkr])[:, None], (LANES, LANES))
    krt_new = jnp.pad(jnp.swapaxes(seq_rows(kr_f), 1, 2), ((0, 0), (0, 0), (0, PAGE - s_len)))
    j_ = np.arange(LANES)[None, :]
    s_of_row = (np.arange(hs) % s_len)[:, None]
    mnew = jnp.asarray(np.where((j_ <= s_of_row) & (j_ < s_len), 0.0, NEG_INF).astype(np.float32))
    krp = jnp.swapaxes(cache_mla_krope, 2, 3)
    dkp = cache_diff_k.reshape(depth, n_pool, PAGE * n_kvheads, 2 * HALF)
    dvp = cache_diff_v.reshape(depth, n_pool, PAGE * n_kvheads, LANES)
    olat, od = _decode_attn(page_table, cache_mla_latent, krp, dkp, dvp, qabs_b, lhs2, qdl,
                            seq_rows(c_f), krt_new, seq_rows(kd_f), seq_rows(vd_f),
                            wukt3.reshape(n_heads * NOPE, kv_lora), cst, cst_new, ggt, tlast, tnew, mnew, lam,
                            layer, dims, pages)
    olat_t = olat.reshape(n_seq, n_heads, s_len, kv_lora).transpose(0, 2, 1, 3).reshape(n_s, n_heads * kv_lora)
    od_t = od.reshape(n_seq, n_kvheads, s_len, per_kv, LANES).transpose(0, 2, 1, 3, 4).reshape(n_s, n_dheads * LANES)
    cat_s = _sample_post(olat_t, od_t, wuv, gsub, nf - n_p, dims, sub_scale)

    wr = jnp.pad(w_router[layer].astype(F32), ((0, 0), (0, LANES - n_experts)))
    wrh, wrl = _split_bf16(wr)
    br = jnp.pad(b_router[layer].astype(F32), (0, LANES - n_experts))[None]
    h1, hn, ti, tw = _merge(cat, cat_s, h_flat, w_o[layer].astype(BF16), g_ffn[layer][None], wrh, wrl, br,
                            n_experts)

    n_valid = batch * l_tot + n_s
    n_items = n_experts + -(-(n_valid * TOP_K) // CAP)
    item_e, item_n, item_start, tok2d, slots = _route(ti, valid, n_experts, n_items)
    eo = _experts(item_e, item_n, item_start, tok2d, hn, w_gate_up, b_gate_up, w_down, b_down, layer)
    out = _combine(slots, h1, tw, eo.reshape(n_items * CAP, d_model))

    y_prompt = out[:n_p].reshape(batch, lp, d_model)[:, N_META:l_tot]
    y_sample = out[n_p:n_p + n_s].reshape(n_seq, s_len, d_model)
    pr = lambda a: a[:n_p].reshape(batch, lp, a.shape[1])[:, :l_tot][None]
    sr = lambda a: a[n_p:n_p + n_s].reshape(n_seq, s_len, a.shape[1])[None]
    kv5 = lambda a: a.reshape(a.shape[:3] + (n_kvheads, a.shape[3] // n_kvheads))
    return (y_prompt, y_sample, pr(c_f), pr(kr_f), kv5(pr(kd_f)), kv5(pr(vd_f)),
            sr(c_f), sr(kr_f), kv5(sr(kd_f)), kv5(sr(vd_f)))
```
